```python
import math
import jax, jax.numpy as jnp
from jax import lax
import numpy as np

D_MODEL = 1024
BATCH = 8
SEQ = 4096
DEPTH = 2

D_PLE = 256
A_HEADS = 4
A_DK = 128
A_DV = 128
A_WIDTH = A_HEADS * A_DK
B_HEADS = 8
B_DH = 64
B_WIDTH = B_HEADS * B_DH
D_MIX = A_WIDTH + B_WIDTH
IN_WIDTHS = (A_WIDTH, A_WIDTH, A_HEADS * A_DV, A_HEADS * A_DV,
             B_WIDTH, B_WIDTH, B_WIDTH, B_WIDTH)
D_IN = sum(IN_WIDTHS)
CHUNK = 64
Q_BLOCK = 128
EPS = 1e-6

kernel_name = "hymba_hgrn2_stickbreaking_trunk"


def rmsnorm(x, g):
    xf = x.astype(jnp.float32)
    y = xf * lax.rsqrt(jnp.mean(xf * xf, axis=-1, keepdims=True) + EPS)
    return (y * g.astype(jnp.float32)).astype(x.dtype)


def head_rmsnorm(o, g):
    B_, S_, H, d = o.shape
    y = o * lax.rsqrt(jnp.mean(o * o, axis=-1, keepdims=True) + EPS)
    return y.reshape(B_, S_, H * d) * g.astype(jnp.float32)


def hgrn2_chunkwise(q, k, v, log_f):
    B_, S_, H, dk = q.shape
    dv = v.shape[-1]
    n = S_ // CHUNK

    def to_chunks(t):
        return t.reshape(B_, n, CHUNK, H, t.shape[-1]).transpose(1, 0, 3, 2, 4)

    causal = jnp.tril(jnp.ones((CHUNK, CHUNK), dtype=bool))[:, :, None]

    def step(state, inp):
        qc, kc, vc, gc = inp
        b = jnp.cumsum(gc, axis=2)
        diff = b[:, :, :, None, :] - b[:, :, None, :, :]
        decay = jnp.where(causal, jnp.exp(jnp.where(causal, diff, 0.0)), 0.0)
        scores = jnp.einsum('bhtk,bhtsk,bhsk->bhts', qc, decay, kc)
        o = (jnp.einsum('bhts,bhsv->bhtv', scores, vc)
             + jnp.einsum('bhtk,bhkv->bhtv', qc * jnp.exp(b), state))
        b_last = b[:, :, -1:, :]
        state = (jnp.exp(b_last[:, :, 0, :, None]) * state
                 + jnp.einsum('bhsk,bhsv->bhkv', kc * jnp.exp(b_last - b), vc))
        return state, o

    s0 = jnp.zeros((B_, H, dk, dv), jnp.float32)
    _, o = lax.scan(step, s0, (to_chunks(q), to_chunks(k), to_chunks(v), to_chunks(log_f)))
    return o.transpose(1, 0, 3, 2, 4).reshape(B_, S_, H, dv)


def stick_breaking(q, k, v):
    S_ = q.shape[2]
    scale = B_DH ** -0.5
    outs = []
    for blk in range(S_ // Q_BLOCK):
        t0 = blk * Q_BLOCK
        t1 = t0 + Q_BLOCK
        z = jnp.einsum('bhtd,bhsd->bhts', q[:, :, t0:t1], k[:, :, :t1]) * scale
        mask = jnp.arange(t1)[None, :] < (t0 + jnp.arange(Q_BLOCK))[:, None]
        log_1m = jnp.where(mask, -jax.nn.softplus(z), 0.0)
        log_rest = lax.cumsum(log_1m, axis=3, reverse=True) - log_1m
        w = jnp.where(mask, jnp.exp(jax.nn.log_sigmoid(z) + log_rest), 0.0)
        outs.append(jnp.einsum('bhts,bhsd->bhtd', w, v[:, :, :t1]))
    return jnp.concatenate(outs, axis=2)


def mixer_layer(h, norm_g, w_in, a_norm_g, b_norm_g, w_out, lb):
    B_, S_, _ = h.shape
    f32 = jnp.float32
    u = rmsnorm(h, norm_g)
    proj = jnp.einsum('bsd,de->bse', u, w_in)
    a_q, a_f, a_i, a_g, b_q, b_k, b_v, b_g = jnp.split(
        proj, [int(c) for c in np.cumsum(IN_WIDTHS)[:-1]], axis=-1)

    lb = lb.astype(f32)
    k_a = (1.0 - lb) * jax.nn.sigmoid(-a_f.astype(f32))
    log_f = jnp.log1p(-k_a)
    q_a = jax.nn.silu(a_q.astype(f32))
    hd = lambda t, d: t.reshape(B_, S_, A_HEADS, d)
    o_a = hgrn2_chunkwise(hd(q_a, A_DK), hd(k_a, A_DK), hd(a_i.astype(f32), A_DV), hd(log_f, A_DK))
    o_a = head_rmsnorm(o_a, a_norm_g) * jax.nn.silu(a_g.astype(f32))

    to_heads = lambda t: t.astype(f32).reshape(B_, S_, B_HEADS, B_DH).transpose(0, 2, 1, 3)
    o_b = stick_breaking(to_heads(b_q), to_heads(b_k), to_heads(b_v)).transpose(0, 2, 1, 3)
    o_b = head_rmsnorm(o_b, b_norm_g) * jax.nn.silu(b_g.astype(f32))

    y = jnp.concatenate([o_a, o_b], axis=-1).astype(h.dtype)
    return h + jnp.einsum('bse,ed->bsd', y, w_out)


def setup_inputs(seed: int = 0) -> dict:
    key = jax.random.key(seed)
    ks = jax.random.split(key, 14)
    f32 = jnp.float32
    nrm = lambda k, shape, s: jax.random.normal(k, shape, f32) * s
    return {
        "x": nrm(ks[0], (BATCH, SEQ, D_MODEL), 1.0),
        "p": nrm(ks[1], (DEPTH, BATCH, SEQ, D_PLE), 1.0),
        "norm_mix": 1.0 + nrm(ks[2], (DEPTH, D_MODEL), 0.02),
        "w_in": nrm(ks[3], (DEPTH, D_MODEL, D_IN), D_MODEL ** -0.5),
        "a_out_norm": 1.0 + nrm(ks[4], (DEPTH, A_HEADS * A_DV), 0.02),
        "b_out_norm": 1.0 + nrm(ks[5], (DEPTH, B_WIDTH), 0.02),
        "w_out": nrm(ks[6], (DEPTH, D_MIX, D_MODEL), 0.5 * D_MIX ** -0.5),
        "lb_logits": nrm(ks[7], (DEPTH, A_WIDTH), 0.1),
        "ple_gate_norm": 1.0 + nrm(ks[8], (DEPTH, D_MODEL), 0.02),
        "w_ple_gate": nrm(ks[9], (DEPTH, D_MODEL, D_MODEL), D_MODEL ** -0.5),
        "w_ple_proj": nrm(ks[10], (DEPTH, D_PLE, D_MODEL), D_PLE ** -0.5),
        "ple_post_norm": 1.0 + nrm(ks[11], (DEPTH, D_MODEL), 0.02),
        "final_norm": 1.0 + nrm(ks[12], (D_MODEL,), 0.02),
    }


def reference(x, p, norm_mix, w_in, a_out_norm, b_out_norm, w_out, lb_logits,
              ple_gate_norm, w_ple_gate, w_ple_proj, ple_post_norm, final_norm):
    sm = jax.nn.softmax(lb_logits.astype(jnp.float32), axis=0)
    lower_bounds = jnp.cumsum(sm, axis=0) - sm[0:1]

    h = x
    for i in range(DEPTH):
        h = mixer_layer(h, norm_mix[i], w_in[i], a_out_norm[i], b_out_norm[i], w_out[i], lower_bounds[i])
        pe = rmsnorm(jnp.einsum('bsc,cd->bsd', p[i], w_ple_proj[i]), ple_post_norm[i])
        gate = jax.nn.sigmoid(jnp.einsum('bsd,de->bse', rmsnorm(h, ple_gate_norm[i]), w_ple_gate[i]))
        h = h + gate * pe
    return rmsnorm(h, final_norm)
```

```python
import functools

import jax
import jax.numpy as jnp
from jax import lax
from jax.experimental import pallas as pl
from jax.experimental.pallas import tpu as pltpu

D_MODEL = 1024
D_PLE = 256
A_HEADS = 4
A_DK = 128
A_WIDTH = A_HEADS * A_DK
B_HEADS = 8
B_DH = 64
B_WIDTH = B_HEADS * B_DH
GROUP = 512
N_GROUPS = 8
CHUNK = 64
SUB = 16
EPS = 1e-6

F32 = jnp.float32
BF16 = jnp.bfloat16

TM_PROJ = 512
T_HGRN = 512
QB = 128
VMEM_LIMIT = 56 * 1024 * 1024


def _sigmoid(x):
    return 1.0 / (1.0 + jnp.exp(-x))


def _silu(x):
    return x * _sigmoid(x)


def _rms(x, g):
    return x * lax.rsqrt(jnp.mean(x * x, axis=-1, keepdims=True) + EPS) * g


def _inproj_kernel(x_ref, g_ref, w_ref, aq_ref, af_ref, ai_ref, ag_ref, bq_ref, bkt_ref, bv_ref, bg_ref):
    u = _rms(x_ref[0], g_ref[...]).astype(BF16)

    def proj(g):
        return jnp.dot(u, w_ref[:, g * GROUP:(g + 1) * GROUP], preferred_element_type=F32)

    aq_ref[0] = _silu(proj(0)).astype(BF16)
    af_ref[0] = proj(1)
    ai_ref[0] = proj(2).astype(BF16)
    ag_ref[0] = _silu(proj(3)).astype(BF16)
    bq_ref[0] = (proj(4) * (B_DH ** -0.5)).astype(BF16)
    bkt_ref[0] = proj(5).T.astype(BF16)
    bv_ref[0] = proj(6).astype(BF16)
    bg_ref[0] = _silu(proj(7)).astype(BF16)


def _inproj(h, norm_g, w_in_bf16):
    B, S, D = h.shape
    tok = lambda dt: jax.ShapeDtypeStruct((B, S, GROUP), dt)
    row_spec = pl.BlockSpec((1, TM_PROJ, GROUP), lambda b, i: (b, i, 0))
    return pl.pallas_call(
        _inproj_kernel,
        grid=(B, S // TM_PROJ),
        in_specs=[
            pl.BlockSpec((1, TM_PROJ, D), lambda b, i: (b, i, 0)),
            pl.BlockSpec((1, D), lambda b, i: (0, 0)),
            pl.BlockSpec((D, N_GROUPS * GROUP), lambda b, i: (0, 0)),
        ],
        out_specs=[row_spec, row_spec, row_spec, row_spec, row_spec,
                   pl.BlockSpec((1, GROUP, TM_PROJ), lambda b, i: (b, 0, i)),
                   row_spec, row_spec],
        out_shape=[tok(BF16), tok(F32), tok(BF16), tok(BF16), tok(BF16),
                   jax.ShapeDtypeStruct((B, GROUP, S), BF16), tok(BF16), tok(BF16)],
        compiler_params=pltpu.CompilerParams(
            dimension_semantics=("parallel", "parallel"), vmem_limit_bytes=VMEM_LIMIT),
        name="inproj",
    )(h, norm_g.reshape(1, D), w_in_bf16)


def _hgrn2_kernel(layer, aq_ref, af_ref, ai_ref, ag_ref, lbl_ref, g_ref, y_ref, st_ref):
    @pl.when(pl.program_id(1) == 0)
    def _():
        st_ref[...] = jnp.zeros_like(st_ref)

    logits = lbl_ref[...]
    e = jnp.exp(logits - jnp.max(logits, axis=0, keepdims=True))
    sm = e / jnp.sum(e, axis=0, keepdims=True)
    lb = jnp.sum(sm[:layer + 1], axis=0, keepdims=True) - sm[0:1]
    one_minus_lb = 1.0 - lb
    gain = g_ref[...]

    r_i = lax.broadcasted_iota(jnp.int32, (CHUNK, CHUNK), 0)
    c_i = lax.broadcasted_iota(jnp.int32, (CHUNK, CHUNK), 1)
    tril = jnp.where(c_i <= r_i, 1.0, 0.0).astype(BF16)

    def chunk_body(c, carry):
        r0 = pl.multiple_of(c * CHUNK, CHUNK)
        rows = pl.ds(r0, CHUNK)
        for h in range(A_HEADS):
            cols = slice(h * A_DK, (h + 1) * A_DK)
            af = af_ref[0, rows, cols]
            q = aq_ref[0, rows, cols].astype(F32)
            v = ai_ref[0, rows, cols]
            ka = one_minus_lb[:, cols] / (1.0 + jnp.exp(af))
            logf = jnp.log1p(-ka)
            hi = logf.astype(BF16)
            lo = (logf - hi.astype(F32)).astype(BF16)
            b = (jnp.dot(tril, hi, preferred_element_type=F32)
                 + jnp.dot(tril, lo, preferred_element_type=F32))
            b_last = b[CHUNK - 1:CHUNK, :]

            st = st_ref[h]
            qe = (q * jnp.exp(b)).astype(BF16)
            o_inter = lax.dot_general(qe, st.astype(BF16), (((1,), (1,)), ((), ())),
                                      preferred_element_type=F32)

            pieces = []
            for i in range(CHUNK // SUB):
                n = SUB * (i + 1)
                ref_row = b[SUB * i + SUB // 2 - 1:SUB * i + SUB // 2, :]
                qs = (q[SUB * i:n] * jnp.exp(b[SUB * i:n] - ref_row)).astype(BF16)
                ks = (ka[0:n] * jnp.exp(ref_row - b[0:n])).astype(BF16)
                a = lax.dot_general(qs, ks, (((1,), (1,)), ((), ())), preferred_element_type=F32)
                t_idx = SUB * i + lax.broadcasted_iota(jnp.int32, (SUB, n), 0)
                s_idx = lax.broadcasted_iota(jnp.int32, (SUB, n), 1)
                a = jnp.where(s_idx <= t_idx, a, 0.0).astype(BF16)
                pieces.append(jnp.dot(a, v[0:n], preferred_element_type=F32))
            o = o_inter + jnp.concatenate(pieces, axis=0)

            kd = (ka * jnp.exp(b_last - b)).astype(BF16)
            vt = v.astype(F32).T.astype(BF16)
            st_ref[h] = st * jnp.exp(b_last) + jnp.dot(vt, kd, preferred_element_type=F32)

            o = o * lax.rsqrt(jnp.mean(o * o, axis=-1, keepdims=True) + EPS) * gain[:, cols]
            y_ref[0, rows, cols] = (o * ag_ref[0, rows, cols].astype(F32)).astype(BF16)
        return carry

    lax.fori_loop(0, T_HGRN // CHUNK, chunk_body, 0)


def _hgrn2(layer, aq, af, ai, ag, lb_logits, a_norm_g):
    B, S, W = aq.shape
    depth = lb_logits.shape[0]
    tok_spec = pl.BlockSpec((1, T_HGRN, W), lambda b, i: (b, i, 0))
    return pl.pallas_call(
        functools.partial(_hgrn2_kernel, layer),
        grid=(B, S // T_HGRN),
        in_specs=[tok_spec, tok_spec, tok_spec, tok_spec,
                  pl.BlockSpec((depth, W), lambda b, i: (0, 0)),
                  pl.BlockSpec((1, W), lambda b, i: (0, 0))],
        out_specs=tok_spec,
        out_shape=jax.ShapeDtypeStruct((B, S, W), BF16),
        scratch_shapes=[pltpu.VMEM((A_HEADS, A_DK, A_DK), F32)],
        compiler_params=pltpu.CompilerParams(
            dimension_semantics=("parallel", "arbitrary"), vmem_limit_bytes=VMEM_LIMIT),
        name="hgrn2",
    )(aq, af, ai, ag, lb_logits, a_norm_g.reshape(1, W))


def _stick_kernel(q_ref, kt_ref, v_ref, gate_ref, g_ref, y_ref, run_ref, acc_ref):
    qi = pl.program_id(2)
    lane = lax.broadcasted_iota(jnp.int32, (QB, 2 * B_DH), 1)
    q2 = q_ref[0]
    zero = jnp.zeros_like(q2)
    q_heads = (jnp.where(lane < B_DH, q2, zero), jnp.where(lane >= B_DH, q2, zero))

    rr = lax.broadcasted_iota(jnp.int32, (2 * QB, 2 * QB), 0) & (QB - 1)
    cc = lax.broadcasted_iota(jnp.int32, (2 * QB, 2 * QB), 1)
    w_cum = jnp.where((cc >= QB) | (rr >= cc), -1.0, 0.0).astype(BF16)

    t_idx = lax.broadcasted_iota(jnp.int32, (QB, QB), 0)
    s_idx = lax.broadcasted_iota(jnp.int32, (QB, QB), 1)
    strictly_causal = s_idx < t_idx

    def tile(j, diag):
        k0 = pl.multiple_of(j * QB, QB)
        kt = kt_ref[0, :, pl.ds(k0, QB)]
        v = v_ref[0, pl.ds(k0, QB), :]
        for h in range(2):
            z = jnp.dot(q_heads[h], kt, preferred_element_type=F32)
            sp = jnp.maximum(z, 0.0) + jnp.log1p(jnp.exp(-jnp.abs(z)))
            if diag:
                sp = jnp.where(strictly_causal, sp, 0.0)
            hi = sp.astype(BF16)
            lo = (sp - hi.astype(F32)).astype(BF16)
            res = jnp.dot(jnp.concatenate([hi, lo], axis=1), w_cum, preferred_element_type=F32)
            cin = res[:, :QB]
            tot = res[:, QB:]
            if diag:
                w = jnp.where(strictly_causal, jnp.exp(z + cin), 0.0)
                run_ref[h] = tot
                acc_ref[h] = jnp.dot(w.astype(BF16), v, preferred_element_type=F32)
            else:
                run = run_ref[h]
                w = jnp.exp(z + cin + run)
                run_ref[h] = run + tot
                acc_ref[h] += jnp.dot(w.astype(BF16), v, preferred_element_type=F32)

    tile(qi, True)

    def body(i, carry):
        tile(qi - 1 - i, False)
        return carry

    lax.fori_loop(0, qi, body, 0)

    first = lane < B_DH
    o = jnp.where(first, acc_ref[0], acc_ref[1])
    o2 = o * o
    ms0 = jnp.sum(jnp.where(first, o2, 0.0), axis=-1, keepdims=True)
    ms1 = jnp.sum(jnp.where(first, 0.0, o2), axis=-1, keepdims=True)
    ms = jnp.where(first, ms0, ms1) * (1.0 / B_DH)
    y = o * lax.rsqrt(ms + EPS) * g_ref[...] * gate_ref[0].astype(F32)
    y_ref[0] = y.astype(BF16)


def _stick(bq, bkt, bv, bg, b_norm_g):
    B, S, W = bq.shape
    pairs = W // (2 * B_DH)
    blk = pl.BlockSpec((1, QB, 2 * B_DH), lambda b, p, i: (b, i, p))
    return pl.pallas_call(
        _stick_kernel,
        grid=(B, pairs, S // QB),
        in_specs=[blk,
                  pl.BlockSpec((1, 2 * B_DH, S), lambda b, p, i: (b, p, 0)),
                  pl.BlockSpec((1, S, 2 * B_DH), lambda b, p, i: (b, 0, p)),
                  blk,
                  pl.BlockSpec((1, 2 * B_DH), lambda b, p, i: (0, p))],
        out_specs=blk,
        out_shape=jax.ShapeDtypeStruct((B, S, W), BF16),
        scratch_shapes=[pltpu.VMEM((2, QB, QB), F32), pltpu.VMEM((2, QB, 2 * B_DH), F32)],
        compiler_params=pltpu.CompilerParams(
            dimension_semantics=("parallel", "parallel", "arbitrary"), vmem_limit_bytes=VMEM_LIMIT),
        name="stick",
    )(bq, bkt, bv, bg, b_norm_g.reshape(1, W))


def _outproj_kernel(last, h_ref, ya_ref, yb_ref, wo_ref, p_ref, wp_ref, post_ref, gn_ref, wg_ref, fin_ref, o_ref):
    h1 = (h_ref[0]
          + jnp.dot(ya_ref[0], wo_ref[:A_WIDTH, :], preferred_element_type=F32)
          + jnp.dot(yb_ref[0], wo_ref[A_WIDTH:, :], preferred_element_type=F32))
    pe = _rms(jnp.dot(p_ref[0].astype(BF16), wp_ref[...], preferred_element_type=F32), post_ref[...])
    r = _rms(h1, gn_ref[...]).astype(BF16)
    gate = _sigmoid(jnp.dot(r, wg_ref[...], preferred_element_type=F32))
    h2 = h1 + gate * pe
    if last:
        h2 = _rms(h2, fin_ref[...])
    o_ref[0] = h2


def _outproj(last, h, ya, yb, w_out, p, w_proj, post_g, gate_g, w_gate, final_g):
    B, S, D = h.shape
    row = lambda w: pl.BlockSpec((1, TM_PROJ, w), lambda b, i: (b, i, 0))
    full = lambda a: pl.BlockSpec(a.shape, lambda b, i: (0, 0))
    vec = lambda a: a.reshape(1, D)
    args = (h, ya, yb, w_out, p, w_proj, vec(post_g), vec(gate_g), w_gate, vec(final_g))
    in_specs = [row(D), row(A_WIDTH), row(B_WIDTH), full(w_out), row(D_PLE), full(w_proj),
                full(args[6]), full(args[7]), full(w_gate), full(args[9])]
    return pl.pallas_call(
        functools.partial(_outproj_kernel, last),
        grid=(B, S // TM_PROJ),
        in_specs=in_specs,
        out_specs=row(D),
        out_shape=jax.ShapeDtypeStruct((B, S, D), F32),
        compiler_params=pltpu.CompilerParams(
            dimension_semantics=("parallel", "parallel"), vmem_limit_bytes=VMEM_LIMIT),
        name="outproj",
    )(*args)


def kernel(x, p, norm_mix, w_in, a_out_norm, b_out_norm, w_out, lb_logits, ple_gate_norm, w_ple_gate,
           w_ple_proj, ple_post_norm, final_norm):
    depth = w_in.shape[0]
    assert x.shape[1] % TM_PROJ == 0 and x.shape[1] % T_HGRN == 0 and x.shape[1] % QB == 0
    w_in_b, w_out_b = w_in.astype(BF16), w_out.astype(BF16)
    w_gate_b, w_proj_b = w_ple_gate.astype(BF16), w_ple_proj.astype(BF16)
    h = x
    for i in range(depth):
        aq, af, ai, ag, bq, bkt, bv, bg = _inproj(h, norm_mix[i], w_in_b[i])
        ya = _hgrn2(i, aq, af, ai, ag, lb_logits, a_out_norm[i])
        yb = _stick(bq, bkt, bv, bg, b_out_norm[i])
        h = _outproj(i == depth - 1, h, ya, yb, w_out_b[i], p[i], w_proj_b[i], ple_post_norm[i],
                     ple_gate_norm[i], w_gate_b[i], final_norm)
    return h
```

```python
import functools
import math

import jax
import jax.numpy as jnp
from jax import lax
from jax.experimental import pallas as pl
from jax.experimental.pallas import tpu as pltpu

D_MODEL = 1024
D_PLE = 256
A_HEADS = 4
A_DK = 128
A_WIDTH = A_HEADS * A_DK
B_HEADS = 8
B_DH = 64
B_WIDTH = B_HEADS * B_DH
GROUP = 512
N_GROUPS = 8
CHUNK = 64
SUB = 16
EPS = 1e-6
LOG2E = math.log2(math.e)

F32 = jnp.float32
BF16 = jnp.bfloat16

TM_PROJ = 512
T_HGRN = 512
SQ = 512
KB = 128
KG = SQ // KB
VMEM_LIMIT = 56 * 1024 * 1024


def _sigmoid(x):
    return 1.0 / (1.0 + jnp.exp(-x))


def _silu(x):
    return x * _sigmoid(x)


def _rms(x, g):
    return x * lax.rsqrt(jnp.mean(x * x, axis=-1, keepdims=True) + EPS) * g


def _inproj_kernel(x_ref, g_ref, w_ref, aq_ref, af_ref, ai_ref, ag_ref, bq_ref, bkt_ref, bv_ref, bg_ref):
    u = _rms(x_ref[0], g_ref[...]).astype(BF16)

    def proj(g):
        return jnp.dot(u, w_ref[:, g * GROUP:(g + 1) * GROUP], preferred_element_type=F32)

    aq_ref[0] = _silu(proj(0)).astype(BF16)
    af_ref[0] = proj(1)
    ai_ref[0] = proj(2).astype(BF16)
    ag_ref[0] = _silu(proj(3)).astype(BF16)
    bq_ref[0] = (proj(4) * (B_DH ** -0.5 * LOG2E)).astype(BF16)
    bkt_ref[0] = proj(5).T.astype(BF16)
    bv_ref[0] = proj(6).astype(BF16)
    bg_ref[0] = _silu(proj(7)).astype(BF16)


def _inproj(h, norm_g, w_in_bf16):
    B, S, D = h.shape
    tok = lambda dt: jax.ShapeDtypeStruct((B, S, GROUP), dt)
    row_spec = pl.BlockSpec((1, TM_PROJ, GROUP), lambda b, i: (b, i, 0))
    return pl.pallas_call(
        _inproj_kernel,
        grid=(B, S // TM_PROJ),
        in_specs=[
            pl.BlockSpec((1, TM_PROJ, D), lambda b, i: (b, i, 0)),
            pl.BlockSpec((1, D), lambda b, i: (0, 0)),
            pl.BlockSpec((D, N_GROUPS * GROUP), lambda b, i: (0, 0)),
        ],
        out_specs=[row_spec, row_spec, row_spec, row_spec, row_spec,
                   pl.BlockSpec((1, GROUP, TM_PROJ), lambda b, i: (b, 0, i)),
                   row_spec, row_spec],
        out_shape=[tok(BF16), tok(F32), tok(BF16), tok(BF16), tok(BF16),
                   jax.ShapeDtypeStruct((B, GROUP, S), BF16), tok(BF16), tok(BF16)],
        compiler_params=pltpu.CompilerParams(
            dimension_semantics=("parallel", "parallel"), vmem_limit_bytes=VMEM_LIMIT),
        name="inproj",
    )(h, norm_g.reshape(1, D), w_in_bf16)


def _hgrn2_kernel(layer, aq_ref, af_ref, ai_ref, ag_ref, lbl_ref, g_ref, y_ref, st_ref):
    @pl.when(pl.program_id(1) == 0)
    def _():
        st_ref[...] = jnp.zeros_like(st_ref)

    logits = lbl_ref[...]
    e = jnp.exp(logits - jnp.max(logits, axis=0, keepdims=True))
    sm = e / jnp.sum(e, axis=0, keepdims=True)
    lb = jnp.sum(sm[:layer + 1], axis=0, keepdims=True) - sm[0:1]
    one_minus_lb = 1.0 - lb
    gain = g_ref[...]

    r_i = lax.broadcasted_iota(jnp.int32, (CHUNK, CHUNK), 0)
    c_i = lax.broadcasted_iota(jnp.int32, (CHUNK, CHUNK), 1)
    tril = jnp.where(c_i <= r_i, 1.0, 0.0).astype(BF16)
    sub_masks = []
    for i in range(CHUNK // SUB):
        n = SUB * (i + 1)
        t_idx = SUB * i + lax.broadcasted_iota(jnp.int32, (SUB, n), 0)
        s_idx = lax.broadcasted_iota(jnp.int32, (SUB, n), 1)
        sub_masks.append(s_idx <= t_idx)

    items = [(h, c) for h in range(A_HEADS) for c in range(T_HGRN // CHUNK)]
    n_sub = CHUNK // SUB
    nt = (((1,), (1,)), ((), ()))

    def rows_of(c):
        return slice(c * CHUNK, (c + 1) * CHUNK)

    def cols_of(h):
        return slice(h * A_DK, (h + 1) * A_DK)

    q, v, ka, hi_lo = {}, {}, {}, {}
    for h, c in items:
        af = af_ref[0, rows_of(c), cols_of(h)]
        q[h, c] = aq_ref[0, rows_of(c), cols_of(h)].astype(F32)
        v[h, c] = ai_ref[0, rows_of(c), cols_of(h)]
        ka[h, c] = one_minus_lb[:, cols_of(h)] / (1.0 + jnp.exp(af))
        logf = jnp.log(1.0 - ka[h, c])
        hi = logf.astype(BF16)
        lo = (logf - hi.astype(F32)).astype(BF16)
        hi_lo[h, c] = jnp.concatenate([hi, lo], axis=0)
    tril2 = jnp.concatenate([tril, tril], axis=1)
    b = {k: jnp.dot(tril2, hi_lo[k], preferred_element_type=F32) for k in items}
    qe, kd, decay, qs, ks = {}, {}, {}, {}, {}
    for k in items:
        b_last = b[k][CHUNK - 1:CHUNK, :]
        qe[k] = (q[k] * jnp.exp(b[k])).astype(BF16)
        kd[k] = (ka[k] * jnp.exp(b_last - b[k])).astype(BF16)
        decay[k] = jnp.exp(b_last)
        for i in range(n_sub):
            n = SUB * (i + 1)
            ref_row = b[k][SUB * i + SUB // 2 - 1:SUB * i + SUB // 2, :]
            qs[k, i] = (q[k][SUB * i:n] * jnp.exp(b[k][SUB * i:n] - ref_row)).astype(BF16)
            ks[k, i] = (ka[k][0:n] * jnp.exp(ref_row - b[k][0:n])).astype(BF16)
    a = {}
    for k in items:
        for i in range(n_sub):
            s = lax.dot_general(qs[k, i], ks[k, i], nt, preferred_element_type=F32)
            a[k, i] = jnp.where(sub_masks[i], s, 0.0).astype(BF16)
    o_intra, upd = {}, {}
    for k in items:
        o_intra[k] = jnp.concatenate(
            [jnp.dot(a[k, i], v[k][0:SUB * (i + 1)], preferred_element_type=F32) for i in range(n_sub)], axis=0)
        vt = v[k].astype(F32).T.astype(BF16)
        upd[k] = jnp.dot(vt, kd[k], preferred_element_type=F32)
    st = [st_ref[h] for h in range(A_HEADS)]
    for c in range(T_HGRN // CHUNK):
        for h in range(A_HEADS):
            o = o_intra[h, c] + lax.dot_general(qe[h, c], st[h].astype(BF16), nt, preferred_element_type=F32)
            st[h] = st[h] * decay[h, c] + upd[h, c]
            o = o * lax.rsqrt(jnp.mean(o * o, axis=-1, keepdims=True) + EPS) * gain[:, cols_of(h)]
            y_ref[0, rows_of(c), cols_of(h)] = (o * ag_ref[0, rows_of(c), cols_of(h)].astype(F32)).astype(BF16)
    for h in range(A_HEADS):
        st_ref[h] = st[h]


def _hgrn2(layer, aq, af, ai, ag, lb_logits, a_norm_g):
    B, S, W = aq.shape
    depth = lb_logits.shape[0]
    tok_spec = pl.BlockSpec((1, T_HGRN, W), lambda b, i: (b, i, 0))
    return pl.pallas_call(
        functools.partial(_hgrn2_kernel, layer),
        grid=(B, S // T_HGRN),
        in_specs=[tok_spec, tok_spec, tok_spec, tok_spec,
                  pl.BlockSpec((depth, W), lambda b, i: (0, 0)),
                  pl.BlockSpec((1, W), lambda b, i: (0, 0))],
        out_specs=tok_spec,
        out_shape=jax.ShapeDtypeStruct((B, S, W), BF16),
        scratch_shapes=[pltpu.VMEM((A_HEADS, A_DK, A_DK), F32)],
        compiler_params=pltpu.CompilerParams(
            dimension_semantics=("parallel", "arbitrary"), vmem_limit_bytes=VMEM_LIMIT),
        name="hgrn2",
    )(aq, af, ai, ag, lb_logits, a_norm_g.reshape(1, W))


def _stick_kernel(q_ref, kt_ref, v_ref, gate_ref, g_ref, y_ref, run_ref, acc_ref):
    qi = pl.program_id(2)
    lane = lax.broadcasted_iota(jnp.int32, (SQ, 2 * B_DH), 1)
    first = lane < B_DH
    q2 = q_ref[0]
    zero = jnp.zeros_like(q2)
    q_heads = (jnp.where(first, q2, zero), jnp.where(first, zero, q2))

    rr = lax.broadcasted_iota(jnp.int32, (2 * KB, 2 * KB), 0)
    cc = lax.broadcasted_iota(jnp.int32, (2 * KB, 2 * KB), 1)
    src_key = jnp.where(rr < KB, rr + KB, rr - KB)
    dst_key = jnp.where(cc < KB, cc + KB, cc - KB)
    w_cum = jnp.where(src_key >= dst_key, -1.0, 0.0).astype(BF16)

    row = lax.broadcasted_iota(jnp.int32, (SQ, KB), 0)
    col = lax.broadcasted_iota(jnp.int32, (SQ, KB), 1)

    def group(j0, diagonal):
        k0 = pl.multiple_of(j0 * KB, KB * KG)
        kt = kt_ref[0, :, pl.ds(k0, KB * KG)]
        v = v_ref[0, pl.ds(k0, KB * KG), :]
        heads = range(2)
        z_all = [jnp.dot(q_heads[h], kt, preferred_element_type=F32) for h in heads]
        run = [None if diagonal else run_ref[h] for h in heads]
        weights = [[None] * KG for _ in heads]
        for pair in reversed(range(KG // 2)):
            blocks = (2 * pair + 1, 2 * pair)
            masks = [col + (m * KB) < row if diagonal else None for m in blocks]
            z = [[z_all[h][:, m * KB:(m + 1) * KB] for m in blocks] for h in heads]
            cum, tot = [None, None], [None, None]
            for h in heads:
                sp = []
                for i in range(2):
                    neg_abs = lax.bitcast_convert_type(
                        lax.bitcast_convert_type(z[h][i], jnp.int32) | jnp.int32(-2 ** 31), F32)
                    s = jnp.maximum(z[h][i], 0.0) + jnp.log(1.0 + jnp.exp2(neg_abs)) * LOG2E
                    sp.append(jnp.where(masks[i], s, 0.0) if diagonal else s)
                tot[h] = jnp.sum(sp[0] + sp[1], axis=-1, keepdims=True)
                cum[h] = jnp.dot(jnp.concatenate([sp[0].astype(BF16), sp[1].astype(BF16)], axis=1), w_cum,
                                 preferred_element_type=F32)
            for h in heads:
                for i, m in enumerate(blocks):
                    arg = z[h][i] + cum[h][:, i * KB:(i + 1) * KB]
                    if run[h] is not None:
                        arg = arg + run[h]
                    w = jnp.exp2(arg)
                    if diagonal:
                        w = jnp.where(masks[i], w, 0.0)
                    weights[h][m] = w.astype(BF16)
                run[h] = jnp.broadcast_to(-tot[h], (SQ, KB)) if run[h] is None else run[h] - tot[h]
        for h in heads:
            pv = jnp.dot(jnp.concatenate(weights[h], axis=1), v, preferred_element_type=F32)
            run_ref[h] = run[h]
            if diagonal:
                acc_ref[h] = pv
            else:
                acc_ref[h] += pv

    group(qi * KG, True)

    def body(i, carry):
        group((qi - 1 - i) * KG, False)
        return carry

    lax.fori_loop(0, qi, body, 0)

    o = jnp.where(first, acc_ref[0], acc_ref[1])
    o2 = o * o
    ms0 = jnp.sum(jnp.where(first, o2, 0.0), axis=-1, keepdims=True)
    ms1 = jnp.sum(jnp.where(first, 0.0, o2), axis=-1, keepdims=True)
    ms = jnp.where(first, ms0, ms1) * (1.0 / B_DH)
    y = o * lax.rsqrt(ms + EPS) * g_ref[...] * gate_ref[0].astype(F32)
    y_ref[0] = y.astype(BF16)


def _stick(bq, bkt, bv, bg, b_norm_g):
    B, S, W = bq.shape
    pairs = W // (2 * B_DH)
    blk = pl.BlockSpec((1, SQ, 2 * B_DH), lambda b, p, i: (b, i, p))
    return pl.pallas_call(
        _stick_kernel,
        grid=(B, pairs, S // SQ),
        in_specs=[blk,
                  pl.BlockSpec((1, 2 * B_DH, S), lambda b, p, i: (b, p, 0)),
                  pl.BlockSpec((1, S, 2 * B_DH), lambda b, p, i: (b, 0, p)),
                  blk,
                  pl.BlockSpec((1, 2 * B_DH), lambda b, p, i: (0, p))],
        out_specs=blk,
        out_shape=jax.ShapeDtypeStruct((B, S, W), BF16),
        scratch_shapes=[pltpu.VMEM((2, SQ, KB), F32), pltpu.VMEM((2, SQ, 2 * B_DH), F32)],
        compiler_params=pltpu.CompilerParams(
            dimension_semantics=("parallel", "parallel", "arbitrary"), vmem_limit_bytes=VMEM_LIMIT),
        name="stick",
    )(bq, bkt, bv, bg, b_norm_g.reshape(1, W))


def _outproj_kernel(last, h_ref, ya_ref, yb_ref, wo_ref, p_ref, wp_ref, post_ref, gn_ref, wg_ref, fin_ref, o_ref):
    h1 = (h_ref[0]
          + jnp.dot(ya_ref[0], wo_ref[:A_WIDTH, :], preferred_element_type=F32)
          + jnp.dot(yb_ref[0], wo_ref[A_WIDTH:, :], preferred_element_type=F32))
    pe = _rms(jnp.dot(p_ref[0].astype(BF16), wp_ref[...], preferred_element_type=F32), post_ref[...])
    r = _rms(h1, gn_ref[...]).astype(BF16)
    gate = _sigmoid(jnp.dot(r, wg_ref[...], preferred_element_type=F32))
    h2 = h1 + gate * pe
    if last:
        h2 = _rms(h2, fin_ref[...])
    o_ref[0] = h2


def _outproj(last, h, ya, yb, w_out, p, w_proj, post_g, gate_g, w_gate, final_g):
    B, S, D = h.shape
    row = lambda w: pl.BlockSpec((1, TM_PROJ, w), lambda b, i: (b, i, 0))
    full = lambda a: pl.BlockSpec(a.shape, lambda b, i: (0, 0))
    vec = lambda a: a.reshape(1, D)
    args = (h, ya, yb, w_out, p, w_proj, vec(post_g), vec(gate_g), w_gate, vec(final_g))
    in_specs = [row(D), row(A_WIDTH), row(B_WIDTH), full(w_out), row(D_PLE), full(w_proj),
                full(args[6]), full(args[7]), full(w_gate), full(args[9])]
    return pl.pallas_call(
        functools.partial(_outproj_kernel, last),
        grid=(B, S // TM_PROJ),
        in_specs=in_specs,
        out_specs=row(D),
        out_shape=jax.ShapeDtypeStruct((B, S, D), F32),
        compiler_params=pltpu.CompilerParams(
            dimension_semantics=("parallel", "parallel"), vmem_limit_bytes=VMEM_LIMIT),
        name="outproj",
    )(*args)


def kernel(x, p, norm_mix, w_in, a_out_norm, b_out_norm, w_out, lb_logits, ple_gate_norm, w_ple_gate,
           w_ple_proj, ple_post_norm, final_norm):
    depth = w_in.shape[0]
    assert x.shape[1] % TM_PROJ == 0 and x.shape[1] % T_HGRN == 0 and x.shape[1] % SQ == 0
    w_in_b, w_out_b = w_in.astype(BF16), w_out.astype(BF16)
    w_gate_b, w_proj_b = w_ple_gate.astype(BF16), w_ple_proj.astype(BF16)
    h = x
    for i in range(depth):
        aq, af, ai, ag, bq, bkt, bv, bg = _inproj(h, norm_mix[i], w_in_b[i])
        ya = _hgrn2(i, aq, af, ai, ag, lb_logits, a_out_norm[i])
        yb = _stick(bq, bkt, bv, bg, b_norm_g=b_out_norm[i])
        h = _outproj(i == depth - 1, h, ya, yb, w_out_b[i], p[i], w_proj_b[i], ple_post_norm[i],
                     ple_gate_norm[i], w_gate_b[i], final_norm)
    return h
```

```python
import functools
import math

import jax
import jax.numpy as jnp
from jax import lax
from jax.experimental import pallas as pl
from jax.experimental.pallas import tpu as pltpu

D_MODEL = 1024
D_PLE = 256
A_HEADS = 4
A_DK = 128
A_WIDTH = A_HEADS * A_DK
B_HEADS = 8
B_DH = 64
B_WIDTH = B_HEADS * B_DH
GROUP = 512
N_GROUPS = 8
CHUNK = 64
SUB = 16
EPS = 1e-6
LOG2E = math.log2(math.e)

F32 = jnp.float32
BF16 = jnp.bfloat16

TM_PROJ = 512
T_HGRN = 512
SQ = 512
KB = 128
KG = SQ // KB
VMEM_LIMIT = 56 * 1024 * 1024


def _sigmoid(x):
    return 1.0 / (1.0 + jnp.exp(-x))


def _silu(x):
    return x * _sigmoid(x)


def _rms(x, g):
    return x * lax.rsqrt(jnp.mean(x * x, axis=-1, keepdims=True) + EPS) * g


def _inproj_kernel(x_ref, g_ref, w_ref, aq_ref, af_ref, ai_ref, ag_ref, bq_ref, bkt_ref, bv_ref, bg_ref):
    u = _rms(x_ref[0], g_ref[...]).astype(BF16)

    def proj(g):
        return jnp.dot(u, w_ref[:, g * GROUP:(g + 1) * GROUP], preferred_element_type=F32)

    aq_ref[0] = _silu(proj(0)).astype(BF16)
    af_ref[0] = proj(1)
    ai_ref[0] = proj(2).astype(BF16)
    ag_ref[0] = _silu(proj(3)).astype(BF16)
    bq_ref[0] = (proj(4) * (B_DH ** -0.5 * LOG2E)).astype(BF16)
    bkt_ref[0] = proj(5).T.astype(BF16)
    bv_ref[0] = proj(6).astype(BF16)
    bg_ref[0] = _silu(proj(7)).astype(BF16)


def _inproj(h, norm_g, w_in_bf16):
    B, S, D = h.shape
    tok = lambda dt: jax.ShapeDtypeStruct((B, S, GROUP), dt)
    row_spec = pl.BlockSpec((1, TM_PROJ, GROUP), lambda b, i: (b, i, 0))
    return pl.pallas_call(
        _inproj_kernel,
        grid=(B, S // TM_PROJ),
        in_specs=[
            pl.BlockSpec((1, TM_PROJ, D), lambda b, i: (b, i, 0)),
            pl.BlockSpec((1, D), lambda b, i: (0, 0)),
            pl.BlockSpec((D, N_GROUPS * GROUP), lambda b, i: (0, 0)),
        ],
        out_specs=[row_spec, row_spec, row_spec, row_spec, row_spec,
                   pl.BlockSpec((1, GROUP, TM_PROJ), lambda b, i: (b, 0, i)),
                   row_spec, row_spec],
        out_shape=[tok(BF16), tok(F32), tok(BF16), tok(BF16), tok(BF16),
                   jax.ShapeDtypeStruct((B, GROUP, S), BF16), tok(BF16), tok(BF16)],
        compiler_params=pltpu.CompilerParams(
            dimension_semantics=("parallel", "parallel"), vmem_limit_bytes=VMEM_LIMIT),
        name="inproj",
    )(h, norm_g.reshape(1, D), w_in_bf16)


def _hgrn2_kernel(layer, aq_ref, af_ref, ai_ref, ag_ref, lbl_ref, g_ref, y_ref, st_ref):
    @pl.when(pl.program_id(1) == 0)
    def _():
        st_ref[...] = jnp.zeros_like(st_ref)

    logits = lbl_ref[...]
    e = jnp.exp(logits - jnp.max(logits, axis=0, keepdims=True))
    sm = e / jnp.sum(e, axis=0, keepdims=True)
    lb = jnp.sum(sm[:layer + 1], axis=0, keepdims=True) - sm[0:1]
    one_minus_lb = 1.0 - lb
    gain = g_ref[...]

    r_i = lax.broadcasted_iota(jnp.int32, (CHUNK, CHUNK), 0)
    c_i = lax.broadcasted_iota(jnp.int32, (CHUNK, CHUNK), 1)
    tril = jnp.where(c_i <= r_i, 1.0, 0.0).astype(BF16)
    sub_masks = []
    for i in range(CHUNK // SUB):
        n = SUB * (i + 1)
        t_idx = SUB * i + lax.broadcasted_iota(jnp.int32, (SUB, n), 0)
        s_idx = lax.broadcasted_iota(jnp.int32, (SUB, n), 1)
        sub_masks.append(s_idx <= t_idx)

    items = [(h, c) for h in range(A_HEADS) for c in range(T_HGRN // CHUNK)]
    n_sub = CHUNK // SUB
    nt = (((1,), (1,)), ((), ()))

    def rows_of(c):
        return slice(c * CHUNK, (c + 1) * CHUNK)

    def cols_of(h):
        return slice(h * A_DK, (h + 1) * A_DK)

    q, v, ka, hi_lo = {}, {}, {}, {}
    for h, c in items:
        af = af_ref[0, rows_of(c), cols_of(h)]
        q[h, c] = aq_ref[0, rows_of(c), cols_of(h)].astype(F32)
        v[h, c] = ai_ref[0, rows_of(c), cols_of(h)]
        ka[h, c] = one_minus_lb[:, cols_of(h)] / (1.0 + jnp.exp(af))
        logf = jnp.log(1.0 - ka[h, c])
        hi = logf.astype(BF16)
        lo = (logf - hi.astype(F32)).astype(BF16)
        hi_lo[h, c] = jnp.concatenate([hi, lo], axis=0)
    tril2 = jnp.concatenate([tril, tril], axis=1)
    b = {k: jnp.dot(tril2, hi_lo[k], preferred_element_type=F32) for k in items}
    qe, kd, decay, qs, ks = {}, {}, {}, {}, {}
    for k in items:
        b_last = b[k][CHUNK - 1:CHUNK, :]
        qe[k] = (q[k] * jnp.exp(b[k])).astype(BF16)
        kd[k] = (ka[k] * jnp.exp(b_last - b[k])).astype(BF16)
        decay[k] = jnp.exp(b_last)
        for i in range(n_sub):
            n = SUB * (i + 1)
            ref_row = b[k][SUB * i + SUB // 2 - 1:SUB * i + SUB // 2, :]
            qs[k, i] = (q[k][SUB * i:n] * jnp.exp(b[k][SUB * i:n] - ref_row)).astype(BF16)
            ks[k, i] = (ka[k][0:n] * jnp.exp(ref_row - b[k][0:n])).astype(BF16)
    a = {}
    for k in items:
        for i in range(n_sub):
            s = lax.dot_general(qs[k, i], ks[k, i], nt, preferred_element_type=F32)
            a[k, i] = jnp.where(sub_masks[i], s, 0.0).astype(BF16)
    o_intra, upd = {}, {}
    for k in items:
        o_intra[k] = jnp.concatenate(
            [jnp.dot(a[k, i], v[k][0:SUB * (i + 1)], preferred_element_type=F32) for i in range(n_sub)], axis=0)
        vt = v[k].astype(F32).T.astype(BF16)
        upd[k] = jnp.dot(vt, kd[k], preferred_element_type=F32)
    st = [st_ref[h] for h in range(A_HEADS)]
    for c in range(T_HGRN // CHUNK):
        for h in range(A_HEADS):
            o = o_intra[h, c] + lax.dot_general(qe[h, c], st[h].astype(BF16), nt, preferred_element_type=F32)
            st[h] = st[h] * decay[h, c] + upd[h, c]
            o = o * lax.rsqrt(jnp.mean(o * o, axis=-1, keepdims=True) + EPS) * gain[:, cols_of(h)]
            y_ref[0, rows_of(c), cols_of(h)] = (o * ag_ref[0, rows_of(c), cols_of(h)].astype(F32)).astype(BF16)
    for h in range(A_HEADS):
        st_ref[h] = st[h]


def _hgrn2(layer, aq, af, ai, ag, lb_logits, a_norm_g):
    B, S, W = aq.shape
    depth = lb_logits.shape[0]
    tok_spec = pl.BlockSpec((1, T_HGRN, W), lambda b, i: (b, i, 0))
    return pl.pallas_call(
        functools.partial(_hgrn2_kernel, layer),
        grid=(B, S // T_HGRN),
        in_specs=[tok_spec, tok_spec, tok_spec, tok_spec,
                  pl.BlockSpec((depth, W), lambda b, i: (0, 0)),
                  pl.BlockSpec((1, W), lambda b, i: (0, 0))],
        out_specs=tok_spec,
        out_shape=jax.ShapeDtypeStruct((B, S, W), BF16),
        scratch_shapes=[pltpu.VMEM((A_HEADS, A_DK, A_DK), F32)],
        compiler_params=pltpu.CompilerParams(
            dimension_semantics=("parallel", "arbitrary"), vmem_limit_bytes=VMEM_LIMIT),
        name="hgrn2",
    )(aq, af, ai, ag, lb_logits, a_norm_g.reshape(1, W))


def _stick_kernel(q_ref, kt_ref, v_ref, gate_ref, g_ref, y_ref, run_ref, acc_ref):
    qi = pl.program_id(2)
    heads = range(2)
    pairs = tuple(reversed(range(KG // 2)))
    lane = lax.broadcasted_iota(jnp.int32, (SQ, 2 * B_DH), 1)
    first = lane < B_DH
    q2 = q_ref[0]
    zero = jnp.zeros_like(q2)
    q_heads = (jnp.where(first, q2, zero), jnp.where(first, zero, q2))

    src_key = lax.broadcasted_iota(jnp.int32, (2 * KB, 2 * KB), 0)
    dst_key = lax.broadcasted_iota(jnp.int32, (2 * KB, 2 * KB), 1)
    w_cum = jnp.where(src_key >= dst_key, -1.0, 0.0).astype(BF16)

    row = lax.broadcasted_iota(jnp.int32, (SQ, 2 * KB), 0)
    col = lax.broadcasted_iota(jnp.int32, (SQ, 2 * KB), 1)

    def visit(groups, first_is_diagonal):
        z = {}

        def first_row(g, p):
            return 2 * p * KB if first_is_diagonal and g == 0 else 0

        def logits(g, p, h):
            k0 = pl.multiple_of((groups[g] + 2 * p) * KB, 2 * KB)
            kt = kt_ref[0, :, pl.ds(k0, 2 * KB)]
            z[g, p, h] = jnp.dot(q_heads[h][first_row(g, p):], kt, preferred_element_type=F32)

        for p in pairs:
            for h in heads:
                logits(0, p, h)
        run = [run_ref[h] for h in heads]
        acc = [acc_ref[h] for h in heads]
        for g, j0 in enumerate(groups):
            diagonal = first_is_diagonal and g == 0
            w = {}
            for p in pairs:
                r0 = first_row(g, p)
                mask = (col + (2 * p * KB) < row)[r0:] if diagonal else None
                cum, tot = [None, None], [None, None]
                for h in heads:
                    zz = z[g, p, h]
                    sp = jnp.maximum(zz, 0.0) + jnp.log(1.0 + jnp.exp2(-jnp.abs(zz))) * LOG2E
                    if diagonal:
                        sp = jnp.where(mask, sp, 0.0)
                    tot[h] = jnp.sum(sp, axis=-1, keepdims=True)
                    cum[h] = jnp.dot(sp.astype(BF16), w_cum, preferred_element_type=F32)
                    if g + 1 < len(groups):
                        logits(g + 1, p, h)
                for h in heads:
                    live = run[h][r0:]
                    wt = jnp.exp2(z[g, p, h] + cum[h] + jnp.concatenate([live, live], axis=1))
                    if diagonal:
                        wt = jnp.where(mask, wt, 0.0)
                    wt = wt.astype(BF16)
                    live = live - tot[h]
                    if r0:
                        wt = jnp.concatenate([jnp.zeros((r0, 2 * KB), BF16), wt], axis=0)
                        live = jnp.concatenate([run[h][:r0], live], axis=0)
                    w[p, h] = wt
                    run[h] = live
            k0 = pl.multiple_of(j0 * KB, KG * KB)
            v = v_ref[0, pl.ds(k0, KG * KB), :]
            for h in heads:
                w_all = jnp.concatenate([w[p, h] for p in sorted(pairs)], axis=1)
                acc[h] = acc[h] + jnp.dot(w_all, v, preferred_element_type=F32)
        for h in heads:
            run_ref[h] = run[h]
            acc_ref[h] = acc[h]

    run_ref[...] = jnp.zeros_like(run_ref)
    acc_ref[...] = jnp.zeros_like(acc_ref)
    odd = qi % 2

    @pl.when(odd == 0)
    def _():
        visit([qi * KG], True)

    @pl.when(odd == 1)
    def _():
        visit([qi * KG, (qi - 1) * KG], True)

    top = qi - 1 - odd

    def body(i, carry):
        visit([(top - 2 * i) * KG, (top - 2 * i - 1) * KG], False)
        return carry

    lax.fori_loop(0, qi // 2, body, 0)

    o = jnp.where(first, acc_ref[0], acc_ref[1])
    o2 = o * o
    ms0 = jnp.sum(jnp.where(first, o2, 0.0), axis=-1, keepdims=True)
    ms1 = jnp.sum(jnp.where(first, 0.0, o2), axis=-1, keepdims=True)
    ms = jnp.where(first, ms0, ms1) * (1.0 / B_DH)
    y = o * lax.rsqrt(ms + EPS) * g_ref[...] * gate_ref[0].astype(F32)
    y_ref[0] = y.astype(BF16)


def _stick(bq, bkt, bv, bg, b_norm_g):
    B, S, W = bq.shape
    pairs = W // (2 * B_DH)
    blk = pl.BlockSpec((1, SQ, 2 * B_DH), lambda b, p, i: (b, i, p))
    return pl.pallas_call(
        _stick_kernel,
        grid=(B, pairs, S // SQ),
        in_specs=[blk,
                  pl.BlockSpec((1, 2 * B_DH, S), lambda b, p, i: (b, p, 0)),
                  pl.BlockSpec((1, S, 2 * B_DH), lambda b, p, i: (b, 0, p)),
                  blk,
                  pl.BlockSpec((1, 2 * B_DH), lambda b, p, i: (0, p))],
        out_specs=blk,
        out_shape=jax.ShapeDtypeStruct((B, S, W), BF16),
        scratch_shapes=[pltpu.VMEM((2, SQ, KB), F32), pltpu.VMEM((2, SQ, 2 * B_DH), F32)],
        compiler_params=pltpu.CompilerParams(
            dimension_semantics=("parallel", "parallel", "arbitrary"), vmem_limit_bytes=VMEM_LIMIT),
        name="stick",
    )(bq, bkt, bv, bg, b_norm_g.reshape(1, W))


def _outproj_kernel(last, h_ref, ya_ref, yb_ref, wo_ref, p_ref, wp_ref, post_ref, gn_ref, wg_ref, fin_ref, o_ref):
    h1 = (h_ref[0]
          + jnp.dot(ya_ref[0], wo_ref[:A_WIDTH, :], preferred_element_type=F32)
          + jnp.dot(yb_ref[0], wo_ref[A_WIDTH:, :], preferred_element_type=F32))
    pe = _rms(jnp.dot(p_ref[0].astype(BF16), wp_ref[...], preferred_element_type=F32), post_ref[...])
    r = _rms(h1, gn_ref[...]).astype(BF16)
    gate = _sigmoid(jnp.dot(r, wg_ref[...], preferred_element_type=F32))
    h2 = h1 + gate * pe
    if last:
        h2 = _rms(h2, fin_ref[...])
    o_ref[0] = h2


def _outproj(last, h, ya, yb, w_out, p, w_proj, post_g, gate_g, w_gate, final_g):
    B, S, D = h.shape
    row = lambda w: pl.BlockSpec((1, TM_PROJ, w), lambda b, i: (b, i, 0))
    full = lambda a: pl.BlockSpec(a.shape, lambda b, i: (0, 0))
    vec = lambda a: a.reshape(1, D)
    args = (h, ya, yb, w_out, p, w_proj, vec(post_g), vec(gate_g), w_gate, vec(final_g))
    in_specs = [row(D), row(A_WIDTH), row(B_WIDTH), full(w_out), row(D_PLE), full(w_proj),
                full(args[6]), full(args[7]), full(w_gate), full(args[9])]
    return pl.pallas_call(
        functools.partial(_outproj_kernel, last),
        grid=(B, S // TM_PROJ),
        in_specs=in_specs,
        out_specs=row(D),
        out_shape=jax.ShapeDtypeStruct((B, S, D), F32),
        compiler_params=pltpu.CompilerParams(
            dimension_semantics=("parallel", "parallel"), vmem_limit_bytes=VMEM_LIMIT),
        name="outproj",
    )(*args)


def kernel(x, p, norm_mix, w_in, a_out_norm, b_out_norm, w_out, lb_logits, ple_gate_norm, w_ple_gate,
           w_ple_proj, ple_post_norm, final_norm):
    depth = w_in.shape[0]
    assert x.shape[1] % TM_PROJ == 0 and x.shape[1] % T_HGRN == 0 and x.shape[1] % SQ == 0
    w_in_b, w_out_b = w_in.astype(BF16), w_out.astype(BF16)
    w_gate_b, w_proj_b = w_ple_gate.astype(BF16), w_ple_proj.astype(BF16)
    h = x
    for i in range(depth):
        aq, af, ai, ag, bq, bkt, bv, bg = _inproj(h, norm_mix[i], w_in_b[i])
        ya = _hgrn2(i, aq, af, ai, ag, lb_logits, a_out_norm[i])
        yb = _stick(bq, bkt, bv, bg, b_norm_g=b_out_norm[i])
        h = _outproj(i == depth - 1, h, ya, yb, w_out_b[i], p[i], w_proj_b[i], ple_post_norm[i],
                     ple_gate_norm[i], w_gate_b[i], final_norm)
    return h
```

```python
import functools
import math

import jax
import jax.numpy as jnp
from jax import lax
from jax.experimental import pallas as pl
from jax.experimental.pallas import tpu as pltpu

D_MODEL = 1024
D_PLE = 256
A_HEADS = 4
A_DK = 128
A_WIDTH = A_HEADS * A_DK
B_HEADS = 8
B_DH = 64
B_WIDTH = B_HEADS * B_DH
GROUP = 512
N_GROUPS = 8
CHUNK = 64
SUB = 16
EPS = 1e-6
LOG2E = math.log2(math.e)

F32 = jnp.float32
BF16 = jnp.bfloat16

TM_PROJ = 512
ROWS_OUT = 256
T_MIX = 512
SQ = 512
KB = 128
KG = SQ // KB
VMEM_LIMIT = 56 * 1024 * 1024


def _sigmoid(x):
    return 1.0 / (1.0 + jnp.exp(-x))


def _silu(x):
    return x * _sigmoid(x)


def _rms(x, g):
    return x * lax.rsqrt(jnp.mean(x * x, axis=-1, keepdims=True) + EPS) * g


def _mix_in_kernel(layer, x_ref, g_ref, w_ref, lbl_ref, ga_ref, ya_ref, bq_ref, bkt_ref, bv_ref, bg_ref, st_ref):
    @pl.when(pl.program_id(1) == 0)
    def _():
        st_ref[...] = jnp.zeros_like(st_ref)

    u = _rms(x_ref[0], g_ref[...]).astype(BF16)

    def proj(g):
        return jnp.dot(u, w_ref[:, g * GROUP:(g + 1) * GROUP], preferred_element_type=F32)

    aq = _silu(proj(0))
    af = proj(1)
    ai = proj(2).astype(BF16)
    ag = _silu(proj(3))

    logits = lbl_ref[...]
    e = jnp.exp(logits - jnp.max(logits, axis=0, keepdims=True))
    sm = e / jnp.sum(e, axis=0, keepdims=True)
    lb = jnp.sum(sm[:layer + 1], axis=0, keepdims=True) - sm[0:1]
    one_minus_lb = 1.0 - lb
    gain = ga_ref[...]

    r_i = lax.broadcasted_iota(jnp.int32, (CHUNK, CHUNK), 0)
    c_i = lax.broadcasted_iota(jnp.int32, (CHUNK, CHUNK), 1)
    tril = jnp.where(c_i <= r_i, 1.0, 0.0).astype(BF16)
    sub_masks = []
    for i in range(CHUNK // SUB):
        n = SUB * (i + 1)
        t_idx = SUB * i + lax.broadcasted_iota(jnp.int32, (SUB, n), 0)
        s_idx = lax.broadcasted_iota(jnp.int32, (SUB, n), 1)
        sub_masks.append(s_idx <= t_idx)

    items = [(h, c) for h in range(A_HEADS) for c in range(T_MIX // CHUNK)]
    n_sub = CHUNK // SUB
    nt = (((1,), (1,)), ((), ()))

    def rows_of(c):
        return slice(c * CHUNK, (c + 1) * CHUNK)

    def cols_of(h):
        return slice(h * A_DK, (h + 1) * A_DK)

    q, v, ka, hi_lo = {}, {}, {}, {}
    for h, c in items:
        q[h, c] = aq[rows_of(c), cols_of(h)]
        v[h, c] = ai[rows_of(c), cols_of(h)]
        ka[h, c] = one_minus_lb[:, cols_of(h)] / (1.0 + jnp.exp(af[rows_of(c), cols_of(h)]))
        logf = jnp.log(1.0 - ka[h, c])
        hi = logf.astype(BF16)
        lo = (logf - hi.astype(F32)).astype(BF16)
        hi_lo[h, c] = jnp.concatenate([hi, lo], axis=0)

    bq_ref[0] = (proj(4) * (B_DH ** -0.5 * LOG2E)).astype(BF16)

    tril2 = jnp.concatenate([tril, tril], axis=1)
    b = {k: jnp.dot(tril2, hi_lo[k], preferred_element_type=F32) for k in items}

    bkt_ref[0] = proj(5).T.astype(BF16)

    qe, kd, decay, qs, ks = {}, {}, {}, {}, {}
    for k in items:
        b_last = b[k][CHUNK - 1:CHUNK, :]
        qe[k] = (q[k] * jnp.exp(b[k])).astype(BF16)
        kd[k] = (ka[k] * jnp.exp(b_last - b[k])).astype(BF16)
        decay[k] = jnp.exp(b_last)
        for i in range(n_sub):
            n = SUB * (i + 1)
            ref_row = b[k][SUB * i + SUB // 2 - 1:SUB * i + SUB // 2, :]
            qs[k, i] = (q[k][SUB * i:n] * jnp.exp(b[k][SUB * i:n] - ref_row)).astype(BF16)
            ks[k, i] = (ka[k][0:n] * jnp.exp(ref_row - b[k][0:n])).astype(BF16)
    a = {}
    for k in items:
        for i in range(n_sub):
            s = lax.dot_general(qs[k, i], ks[k, i], nt, preferred_element_type=F32)
            a[k, i] = jnp.where(sub_masks[i], s, 0.0).astype(BF16)

    bv_ref[0] = proj(6).astype(BF16)

    o_intra, upd = {}, {}
    for k in items:
        o_intra[k] = jnp.concatenate(
            [jnp.dot(a[k, i], v[k][0:SUB * (i + 1)], preferred_element_type=F32) for i in range(n_sub)], axis=0)
        vt = v[k].astype(F32).T.astype(BF16)
        upd[k] = jnp.dot(vt, kd[k], preferred_element_type=F32)

    bg_ref[0] = _silu(proj(7)).astype(BF16)

    st = [st_ref[h] for h in range(A_HEADS)]
    for c in range(T_MIX // CHUNK):
        for h in range(A_HEADS):
            o = o_intra[h, c] + lax.dot_general(qe[h, c], st[h].astype(BF16), nt, preferred_element_type=F32)
            st[h] = st[h] * decay[h, c] + upd[h, c]
            o = o * lax.rsqrt(jnp.mean(o * o, axis=-1, keepdims=True) + EPS) * gain[:, cols_of(h)]
            ya_ref[0, rows_of(c), cols_of(h)] = (o * ag[rows_of(c), cols_of(h)]).astype(BF16)
    for h in range(A_HEADS):
        st_ref[h] = st[h]


def _mix_in(layer, h, norm_g, w_in_bf16, lb_logits, a_norm_g):
    B, S, D = h.shape
    depth = lb_logits.shape[0]
    tok = jax.ShapeDtypeStruct((B, S, GROUP), BF16)
    row_spec = pl.BlockSpec((1, T_MIX, GROUP), lambda b, i: (b, i, 0))
    full = lambda shape: pl.BlockSpec(shape, lambda b, i: (0, 0))
    return pl.pallas_call(
        functools.partial(_mix_in_kernel, layer),
        grid=(B, S // T_MIX),
        in_specs=[pl.BlockSpec((1, T_MIX, D), lambda b, i: (b, i, 0)),
                  full((1, D)), full((D, N_GROUPS * GROUP)), full((depth, A_WIDTH)), full((1, A_WIDTH))],
        out_specs=[row_spec, row_spec,
                   pl.BlockSpec((1, GROUP, T_MIX), lambda b, i: (b, 0, i)),
                   row_spec, row_spec],
        out_shape=[tok, tok, jax.ShapeDtypeStruct((B, GROUP, S), BF16), tok, tok],
        scratch_shapes=[pltpu.VMEM((A_HEADS, A_DK, A_DK), F32)],
        compiler_params=pltpu.CompilerParams(
            dimension_semantics=("parallel", "arbitrary"), vmem_limit_bytes=VMEM_LIMIT),
        name="mix_in",
    )(h, norm_g.reshape(1, D), w_in_bf16, lb_logits, a_norm_g.reshape(1, A_WIDTH))


def _stick_kernel(q_ref, kt_ref, v_ref, gate_ref, g_ref, y_ref, run_ref, acc_ref):
    qi = pl.program_id(2)
    heads = range(2)
    pairs = tuple(reversed(range(KG // 2)))
    lane = lax.broadcasted_iota(jnp.int32, (SQ, 2 * B_DH), 1)
    first = lane < B_DH
    q2 = q_ref[0]
    zero = jnp.zeros_like(q2)
    q_heads = (jnp.where(first, q2, zero), jnp.where(first, zero, q2))

    src_key = lax.broadcasted_iota(jnp.int32, (2 * KB, 2 * KB), 0)
    dst_key = lax.broadcasted_iota(jnp.int32, (2 * KB, 2 * KB), 1)
    w_cum = jnp.where(src_key >= dst_key, -1.0, 0.0).astype(BF16)

    row = lax.broadcasted_iota(jnp.int32, (SQ, 2 * KB), 0)
    col = lax.broadcasted_iota(jnp.int32, (SQ, 2 * KB), 1)

    def visit(groups, first_is_diagonal):
        z = {}

        def first_row(g, p):
            return 2 * p * KB if first_is_diagonal and g == 0 else 0

        def logits(g, p, h):
            k0 = pl.multiple_of((groups[g] + 2 * p) * KB, 2 * KB)
            kt = kt_ref[0, :, pl.ds(k0, 2 * KB)]
            z[g, p, h] = jnp.dot(q_heads[h][first_row(g, p):], kt, preferred_element_type=F32)

        for p in pairs:
            for h in heads:
                logits(0, p, h)
        run = [run_ref[h] for h in heads]
        acc = [acc_ref[h] for h in heads]
        for g, j0 in enumerate(groups):
            diagonal = first_is_diagonal and g == 0
            w = {}
            for p in pairs:
                r0 = first_row(g, p)
                mask = (col + (2 * p * KB) < row)[r0:] if diagonal else None
                cum, tot = [None, None], [None, None]
                for h in heads:
                    zz = z[g, p, h]
                    sp = jnp.maximum(zz, 0.0) + jnp.log(1.0 + jnp.exp2(-jnp.abs(zz))) * LOG2E
                    if diagonal:
                        sp = jnp.where(mask, sp, 0.0)
                    tot[h] = jnp.sum(sp, axis=-1, keepdims=True)
                    cum[h] = jnp.dot(sp.astype(BF16), w_cum, preferred_element_type=F32)
                    if g + 1 < len(groups):
                        logits(g + 1, p, h)
                for h in heads:
                    live = run[h][r0:]
                    wt = jnp.exp2(z[g, p, h] + cum[h] + jnp.concatenate([live, live], axis=1))
                    if diagonal:
                        wt = jnp.where(mask, wt, 0.0)
                    wt = wt.astype(BF16)
                    live = live - tot[h]
                    if r0:
                        wt = jnp.concatenate([jnp.zeros((r0, 2 * KB), BF16), wt], axis=0)
                        live = jnp.concatenate([run[h][:r0], live], axis=0)
                    w[p, h] = wt
                    run[h] = live
            k0 = pl.multiple_of(j0 * KB, KG * KB)
            v = v_ref[0, pl.ds(k0, KG * KB), :]
            for h in heads:
                w_all = jnp.concatenate([w[p, h] for p in sorted(pairs)], axis=1)
                acc[h] = acc[h] + jnp.dot(w_all, v, preferred_element_type=F32)
        for h in heads:
            run_ref[h] = run[h]
            acc_ref[h] = acc[h]

    run_ref[...] = jnp.zeros_like(run_ref)
    acc_ref[...] = jnp.zeros_like(acc_ref)
    odd = qi % 2

    @pl.when(odd == 0)
    def _():
        visit([qi * KG], True)

    @pl.when(odd == 1)
    def _():
        visit([qi * KG, (qi - 1) * KG], True)

    top = qi - 1 - odd

    def body(i, carry):
        visit([(top - 2 * i) * KG, (top - 2 * i - 1) * KG], False)
        return carry

    lax.fori_loop(0, qi // 2, body, 0)

    o = jnp.where(first, acc_ref[0], acc_ref[1])
    o2 = o * o
    ms0 = jnp.sum(jnp.where(first, o2, 0.0), axis=-1, keepdims=True)
    ms1 = jnp.sum(jnp.where(first, 0.0, o2), axis=-1, keepdims=True)
    ms = jnp.where(first, ms0, ms1) * (1.0 / B_DH)
    y = o * lax.rsqrt(ms + EPS) * g_ref[...] * gate_ref[0].astype(F32)
    y_ref[0] = y.astype(BF16)


def _stick(bq, bkt, bv, bg, b_norm_g):
    B, S, W = bq.shape
    pairs = W // (2 * B_DH)
    blk = pl.BlockSpec((1, SQ, 2 * B_DH), lambda b, p, i: (b, i, p))
    return pl.pallas_call(
        _stick_kernel,
        grid=(B, pairs, S // SQ),
        in_specs=[blk,
                  pl.BlockSpec((1, 2 * B_DH, S), lambda b, p, i: (b, p, 0)),
                  pl.BlockSpec((1, S, 2 * B_DH), lambda b, p, i: (b, 0, p)),
                  blk,
                  pl.BlockSpec((1, 2 * B_DH), lambda b, p, i: (0, p))],
        out_specs=blk,
        out_shape=jax.ShapeDtypeStruct((B, S, W), BF16),
        scratch_shapes=[pltpu.VMEM((2, SQ, KB), F32), pltpu.VMEM((2, SQ, 2 * B_DH), F32)],
        compiler_params=pltpu.CompilerParams(
            dimension_semantics=("parallel", "parallel", "arbitrary"), vmem_limit_bytes=VMEM_LIMIT),
        name="stick",
    )(bq, bkt, bv, bg, b_norm_g.reshape(1, W))


def _outproj_kernel(last, h_ref, ya_ref, yb_ref, wo_ref, p_ref, wp_ref, post_ref, gn_ref, wg_ref, fin_ref, o_ref):
    chunks = [slice(r * ROWS_OUT, (r + 1) * ROWS_OUT) for r in range(TM_PROJ // ROWS_OUT)]
    h1, pe, gate = [], [], []
    for rows in chunks:
        h1.append(h_ref[0, rows, :]
                  + jnp.dot(ya_ref[0, rows, :], wo_ref[:A_WIDTH, :], preferred_element_type=F32)
                  + jnp.dot(yb_ref[0, rows, :], wo_ref[A_WIDTH:, :], preferred_element_type=F32))
        pe.append(jnp.dot(p_ref[0, rows, :].astype(BF16), wp_ref[...], preferred_element_type=F32))
    for r in range(len(chunks)):
        normed = _rms(h1[r], gn_ref[...]).astype(BF16)
        gate.append(jnp.dot(normed, wg_ref[...], preferred_element_type=F32))
    for r, rows in enumerate(chunks):
        h2 = h1[r] + _sigmoid(gate[r]) * _rms(pe[r], post_ref[...])
        if last:
            h2 = _rms(h2, fin_ref[...])
        o_ref[0, rows, :] = h2


def _outproj(last, h, ya, yb, w_out, p, w_proj, post_g, gate_g, w_gate, final_g):
    B, S, D = h.shape
    row = lambda w: pl.BlockSpec((1, TM_PROJ, w), lambda b, i: (b, i, 0))
    full = lambda a: pl.BlockSpec(a.shape, lambda b, i: (0, 0))
    vec = lambda a: a.reshape(1, D)
    args = (h, ya, yb, w_out, p, w_proj, vec(post_g), vec(gate_g), w_gate, vec(final_g))
    in_specs = [row(D), row(A_WIDTH), row(B_WIDTH), full(w_out), row(D_PLE), full(w_proj),
                full(args[6]), full(args[7]), full(w_gate), full(args[9])]
    return pl.pallas_call(
        functools.partial(_outproj_kernel, last),
        grid=(B, S // TM_PROJ),
        in_specs=in_specs,
        out_specs=row(D),
        out_shape=jax.ShapeDtypeStruct((B, S, D), F32),
        compiler_params=pltpu.CompilerParams(
            dimension_semantics=("parallel", "parallel"), vmem_limit_bytes=VMEM_LIMIT),
        name="outproj",
    )(*args)


def kernel(x, p, norm_mix, w_in, a_out_norm, b_out_norm, w_out, lb_logits, ple_gate_norm, w_ple_gate,
           w_ple_proj, ple_post_norm, final_norm):
    depth = w_in.shape[0]
    assert x.shape[1] % TM_PROJ == 0 and x.shape[1] % T_MIX == 0 and x.shape[1] % SQ == 0 and T_MIX % CHUNK == 0
    w_in_b, w_out_b = w_in.astype(BF16), w_out.astype(BF16)
    w_gate_b, w_proj_b = w_ple_gate.astype(BF16), w_ple_proj.astype(BF16)
    h = x
    for i in range(depth):
        ya, bq, bkt, bv, bg = _mix_in(i, h, norm_mix[i], w_in_b[i], lb_logits, a_out_norm[i])
        yb = _stick(bq, bkt, bv, bg, b_norm_g=b_out_norm[i])
        h = _outproj(i == depth - 1, h, ya, yb, w_out_b[i], p[i], w_proj_b[i], ple_post_norm[i],
                     ple_gate_norm[i], w_gate_b[i], final_norm)
    return h
```

```python
import functools
import math

import jax
import jax.numpy as jnp
from jax import lax
from jax.experimental import pallas as pl
from jax.experimental.pallas import tpu as pltpu

D_MODEL = 1024
D_PLE = 256
A_HEADS = 4
A_DK = 128
A_WIDTH = A_HEADS * A_DK
B_HEADS = 8
B_DH = 64
B_WIDTH = B_HEADS * B_DH
GROUP = 512
N_GROUPS = 8
CHUNK = 64
SUB = 16
EPS = 1e-6
LOG2E = math.log2(math.e)

F32 = jnp.float32
BF16 = jnp.bfloat16

TM_PROJ = 512
ROWS_OUT = 256
T_MIX = 512
SQ = 512
KB = 128
KG = SQ // KB
VMEM_LIMIT = 56 * 1024 * 1024


def _sigmoid(x):
    return 1.0 / (1.0 + jnp.exp(-x))


def _silu(x):
    return x * _sigmoid(x)


def _rms(x, g):
    return x * lax.rsqrt(jnp.mean(x * x, axis=-1, keepdims=True) + EPS) * g


def _mix_in_kernel(layer, x_ref, g_ref, w_ref, lbl_ref, ga_ref, ya_ref, bq_ref, bkt_ref, bv_ref, bg_ref, st_ref):
    @pl.when(pl.program_id(1) == 0)
    def _():
        st_ref[...] = jnp.zeros_like(st_ref)

    u = _rms(x_ref[0], g_ref[...]).astype(BF16)

    def proj(g):
        return jnp.dot(u, w_ref[0, :, g * GROUP:(g + 1) * GROUP].astype(BF16), preferred_element_type=F32)

    aq = _silu(proj(0))
    af = proj(1)
    ai = proj(2).astype(BF16)
    ag = _silu(proj(3))

    logits = lbl_ref[...]
    e = jnp.exp(logits - jnp.max(logits, axis=0, keepdims=True))
    sm = e / jnp.sum(e, axis=0, keepdims=True)
    lb = jnp.sum(sm[:layer + 1], axis=0, keepdims=True) - sm[0:1]
    one_minus_lb = 1.0 - lb
    gain = ga_ref[...]

    r_i = lax.broadcasted_iota(jnp.int32, (CHUNK, CHUNK), 0)
    c_i = lax.broadcasted_iota(jnp.int32, (CHUNK, CHUNK), 1)
    tril = jnp.where(c_i <= r_i, 1.0, 0.0).astype(BF16)
    sub_masks = []
    for i in range(CHUNK // SUB):
        n = SUB * (i + 1)
        t_idx = SUB * i + lax.broadcasted_iota(jnp.int32, (SUB, n), 0)
        s_idx = lax.broadcasted_iota(jnp.int32, (SUB, n), 1)
        sub_masks.append(s_idx <= t_idx)

    items = [(h, c) for h in range(A_HEADS) for c in range(T_MIX // CHUNK)]
    n_sub = CHUNK // SUB
    nt = (((1,), (1,)), ((), ()))

    def rows_of(c):
        return slice(c * CHUNK, (c + 1) * CHUNK)

    def cols_of(h):
        return slice(h * A_DK, (h + 1) * A_DK)

    q, v, ka, hi_lo = {}, {}, {}, {}
    for h, c in items:
        q[h, c] = aq[rows_of(c), cols_of(h)]
        v[h, c] = ai[rows_of(c), cols_of(h)]
        ka[h, c] = one_minus_lb[:, cols_of(h)] / (1.0 + jnp.exp(af[rows_of(c), cols_of(h)]))
        logf = jnp.log(1.0 - ka[h, c])
        hi = logf.astype(BF16)
        lo = (logf - hi.astype(F32)).astype(BF16)
        hi_lo[h, c] = jnp.concatenate([hi, lo], axis=0)

    bq_ref[0] = (proj(4) * (B_DH ** -0.5 * LOG2E)).astype(BF16)

    tril2 = jnp.concatenate([tril, tril], axis=1)
    b = {k: jnp.dot(tril2, hi_lo[k], preferred_element_type=F32) for k in items}

    bkt_ref[0] = proj(5).T.astype(BF16)

    qe, kd, decay, qs, ks = {}, {}, {}, {}, {}
    for k in items:
        b_last = b[k][CHUNK - 1:CHUNK, :]
        qe[k] = (q[k] * jnp.exp(b[k])).astype(BF16)
        kd[k] = (ka[k] * jnp.exp(b_last - b[k])).astype(BF16)
        decay[k] = jnp.exp(b_last)
        for i in range(n_sub):
            n = SUB * (i + 1)
            ref_row = b[k][SUB * i + SUB // 2 - 1:SUB * i + SUB // 2, :]
            qs[k, i] = (q[k][SUB * i:n] * jnp.exp(b[k][SUB * i:n] - ref_row)).astype(BF16)
            ks[k, i] = (ka[k][0:n] * jnp.exp(ref_row - b[k][0:n])).astype(BF16)
    a = {}
    for k in items:
        for i in range(n_sub):
            s = lax.dot_general(qs[k, i], ks[k, i], nt, preferred_element_type=F32)
            a[k, i] = jnp.where(sub_masks[i], s, 0.0).astype(BF16)

    bv_ref[0] = proj(6).astype(BF16)

    o_intra, upd = {}, {}
    for k in items:
        o_intra[k] = jnp.concatenate(
            [jnp.dot(a[k, i], v[k][0:SUB * (i + 1)], preferred_element_type=F32) for i in range(n_sub)], axis=0)
        vt = v[k].astype(F32).T.astype(BF16)
        upd[k] = jnp.dot(vt, kd[k], preferred_element_type=F32)

    bg_ref[0] = _silu(proj(7)).astype(BF16)

    st = [st_ref[h] for h in range(A_HEADS)]
    for c in range(T_MIX // CHUNK):
        for h in range(A_HEADS):
            o = o_intra[h, c] + lax.dot_general(qe[h, c], st[h].astype(BF16), nt, preferred_element_type=F32)
            st[h] = st[h] * decay[h, c] + upd[h, c]
            o = o * lax.rsqrt(jnp.mean(o * o, axis=-1, keepdims=True) + EPS) * gain[:, cols_of(h)]
            ya_ref[0, rows_of(c), cols_of(h)] = (o * ag[rows_of(c), cols_of(h)]).astype(BF16)
    for h in range(A_HEADS):
        st_ref[h] = st[h]


def _mix_in(layer, h, norm_g, w_in, lb_logits, a_norm_g):
    B, S, D = h.shape
    depth = lb_logits.shape[0]
    tok = jax.ShapeDtypeStruct((B, S, GROUP), BF16)
    row_spec = pl.BlockSpec((1, T_MIX, GROUP), lambda b, i: (b, i, 0))
    full = lambda shape: pl.BlockSpec(shape, lambda b, i: (0, 0))
    return pl.pallas_call(
        functools.partial(_mix_in_kernel, layer),
        grid=(B, S // T_MIX),
        in_specs=[pl.BlockSpec((1, T_MIX, D), lambda b, i: (b, i, 0)),
                  full((1, D)),
                  pl.BlockSpec((1, D, N_GROUPS * GROUP), lambda b, i: (layer, 0, 0), pipeline_mode=pl.Buffered(1)),
                  full((depth, A_WIDTH)), full((1, A_WIDTH))],
        out_specs=[row_spec, row_spec,
                   pl.BlockSpec((1, GROUP, T_MIX), lambda b, i: (b, 0, i)),
                   row_spec, row_spec],
        out_shape=[tok, tok, jax.ShapeDtypeStruct((B, GROUP, S), BF16), tok, tok],
        scratch_shapes=[pltpu.VMEM((A_HEADS, A_DK, A_DK), F32)],
        compiler_params=pltpu.CompilerParams(
            dimension_semantics=("parallel", "arbitrary"), vmem_limit_bytes=VMEM_LIMIT),
        name="mix_in",
    )(h, norm_g.reshape(1, D), w_in, lb_logits, a_norm_g.reshape(1, A_WIDTH))


def _stick_kernel(q_ref, kt_ref, v_ref, gate_ref, g_ref, y_ref, run_ref, acc_ref):
    qi = pl.program_id(2)
    heads = range(2)
    pairs = tuple(reversed(range(KG // 2)))
    lane = lax.broadcasted_iota(jnp.int32, (SQ, 2 * B_DH), 1)
    first = lane < B_DH
    q2 = q_ref[0]
    zero = jnp.zeros_like(q2)
    q_heads = (jnp.where(first, q2, zero), jnp.where(first, zero, q2))

    src_key = lax.broadcasted_iota(jnp.int32, (2 * KB, 2 * KB), 0)
    dst_key = lax.broadcasted_iota(jnp.int32, (2 * KB, 2 * KB), 1)
    w_cum = jnp.where(src_key >= dst_key, -1.0, 0.0).astype(BF16)

    row = lax.broadcasted_iota(jnp.int32, (SQ, 2 * KB), 0)
    col = lax.broadcasted_iota(jnp.int32, (SQ, 2 * KB), 1)

    def visit(groups, first_is_diagonal):
        z = {}

        def first_row(g, p):
            return 2 * p * KB if first_is_diagonal and g == 0 else 0

        def logits(g, p, h):
            k0 = pl.multiple_of((groups[g] + 2 * p) * KB, 2 * KB)
            kt = kt_ref[0, :, pl.ds(k0, 2 * KB)]
            z[g, p, h] = jnp.dot(q_heads[h][first_row(g, p):], kt, preferred_element_type=F32)

        for p in pairs:
            for h in heads:
                logits(0, p, h)
        run = [run_ref[h] for h in heads]
        acc = [acc_ref[h] for h in heads]
        for g, j0 in enumerate(groups):
            diagonal = first_is_diagonal and g == 0
            w = {}
            for p in pairs:
                r0 = first_row(g, p)
                mask = (col + (2 * p * KB) < row)[r0:] if diagonal else None
                cum, tot = [None, None], [None, None]
                for h in heads:
                    zz = z[g, p, h]
                    sp = jnp.maximum(zz, 0.0) + jnp.log(1.0 + jnp.exp2(-jnp.abs(zz))) * LOG2E
                    if diagonal:
                        sp = jnp.where(mask, sp, 0.0)
                    tot[h] = jnp.sum(sp, axis=-1, keepdims=True)
                    cum[h] = jnp.dot(sp.astype(BF16), w_cum, preferred_element_type=F32)
                    if g + 1 < len(groups):
                        logits(g + 1, p, h)
                for h in heads:
                    live = run[h][r0:]
                    wt = jnp.exp2(z[g, p, h] + cum[h] + jnp.concatenate([live, live], axis=1))
                    if diagonal:
                        wt = jnp.where(mask, wt, 0.0)
                    wt = wt.astype(BF16)
                    live = live - tot[h]
                    if r0:
                        wt = jnp.concatenate([jnp.zeros((r0, 2 * KB), BF16), wt], axis=0)
                        live = jnp.concatenate([run[h][:r0], live], axis=0)
                    w[p, h] = wt
                    run[h] = live
            k0 = pl.multiple_of(j0 * KB, KG * KB)
            v = v_ref[0, pl.ds(k0, KG * KB), :]
            for h in heads:
                w_all = jnp.concatenate([w[p, h] for p in sorted(pairs)], axis=1)
                acc[h] = acc[h] + jnp.dot(w_all, v, preferred_element_type=F32)
        for h in heads:
            run_ref[h] = run[h]
            acc_ref[h] = acc[h]

    run_ref[...] = jnp.zeros_like(run_ref)
    acc_ref[...] = jnp.zeros_like(acc_ref)
    odd = qi % 2

    @pl.when(odd == 0)
    def _():
        visit([qi * KG], True)

    @pl.when(odd == 1)
    def _():
        visit([qi * KG, (qi - 1) * KG], True)

    top = qi - 1 - odd

    def body(i, carry):
        visit([(top - 2 * i) * KG, (top - 2 * i - 1) * KG], False)
        return carry

    lax.fori_loop(0, qi // 2, body, 0)

    o = jnp.where(first, acc_ref[0], acc_ref[1])
    o2 = o * o
    ms0 = jnp.sum(jnp.where(first, o2, 0.0), axis=-1, keepdims=True)
    ms1 = jnp.sum(jnp.where(first, 0.0, o2), axis=-1, keepdims=True)
    ms = jnp.where(first, ms0, ms1) * (1.0 / B_DH)
    y = o * lax.rsqrt(ms + EPS) * g_ref[...] * gate_ref[0].astype(F32)
    y_ref[0] = y.astype(BF16)


def _stick(bq, bkt, bv, bg, b_norm_g):
    B, S, W = bq.shape
    pairs = W // (2 * B_DH)
    blk = pl.BlockSpec((1, SQ, 2 * B_DH), lambda b, p, i: (b, i, p))
    return pl.pallas_call(
        _stick_kernel,
        grid=(B, pairs, S // SQ),
        in_specs=[blk,
                  pl.BlockSpec((1, 2 * B_DH, S), lambda b, p, i: (b, p, 0)),
                  pl.BlockSpec((1, S, 2 * B_DH), lambda b, p, i: (b, 0, p)),
                  blk,
                  pl.BlockSpec((1, 2 * B_DH), lambda b, p, i: (0, p))],
        out_specs=blk,
        out_shape=jax.ShapeDtypeStruct((B, S, W), BF16),
        scratch_shapes=[pltpu.VMEM((2, SQ, KB), F32), pltpu.VMEM((2, SQ, 2 * B_DH), F32)],
        compiler_params=pltpu.CompilerParams(
            dimension_semantics=("parallel", "parallel", "arbitrary"), vmem_limit_bytes=VMEM_LIMIT),
        name="stick",
    )(bq, bkt, bv, bg, b_norm_g.reshape(1, W))


def _outproj_kernel(last, h_ref, ya_ref, yb_ref, wo_ref, p_ref, wp_ref, post_ref, gn_ref, wg_ref, fin_ref, o_ref):
    chunks = [slice(r * ROWS_OUT, (r + 1) * ROWS_OUT) for r in range(TM_PROJ // ROWS_OUT)]
    wo_a = wo_ref[0, :A_WIDTH, :].astype(BF16)
    wo_b = wo_ref[0, A_WIDTH:, :].astype(BF16)
    wp = wp_ref[0].astype(BF16)
    wg = wg_ref[0].astype(BF16)
    h1, pe, gate = [], [], []
    for rows in chunks:
        h1.append(h_ref[0, rows, :]
                  + jnp.dot(ya_ref[0, rows, :], wo_a, preferred_element_type=F32)
                  + jnp.dot(yb_ref[0, rows, :], wo_b, preferred_element_type=F32))
        pe.append(jnp.dot(p_ref[0, 0, rows, :].astype(BF16), wp, preferred_element_type=F32))
    for r in range(len(chunks)):
        normed = _rms(h1[r], gn_ref[...]).astype(BF16)
        gate.append(jnp.dot(normed, wg, preferred_element_type=F32))
    for r, rows in enumerate(chunks):
        h2 = h1[r] + _sigmoid(gate[r]) * _rms(pe[r], post_ref[...])
        if last:
            h2 = _rms(h2, fin_ref[...])
        o_ref[0, rows, :] = h2


def _outproj(layer, last, h, ya, yb, w_out, p, w_proj, post_g, gate_g, w_gate, final_g):
    B, S, D = h.shape
    row = lambda w: pl.BlockSpec((1, TM_PROJ, w), lambda b, i: (b, i, 0))
    full = lambda a: pl.BlockSpec(a.shape, lambda b, i: (0, 0))
    slab = lambda a: pl.BlockSpec((1,) + a.shape[1:], lambda b, i: (layer, 0, 0), pipeline_mode=pl.Buffered(1))
    vec = lambda a: a.reshape(1, D)
    args = (h, ya, yb, w_out, p, w_proj, vec(post_g), vec(gate_g), w_gate, vec(final_g))
    p_spec = pl.BlockSpec((1, 1, TM_PROJ, D_PLE), lambda b, i: (layer, b, i, 0))
    in_specs = [row(D), row(A_WIDTH), row(B_WIDTH), slab(w_out), p_spec, slab(w_proj),
                full(args[6]), full(args[7]), slab(w_gate), full(args[9])]
    return pl.pallas_call(
        functools.partial(_outproj_kernel, last),
        grid=(B, S // TM_PROJ),
        in_specs=in_specs,
        out_specs=row(D),
        out_shape=jax.ShapeDtypeStruct((B, S, D), F32),
        compiler_params=pltpu.CompilerParams(
            dimension_semantics=("parallel", "parallel"), vmem_limit_bytes=VMEM_LIMIT),
        name="outproj",
    )(*args)


def kernel(x, p, norm_mix, w_in, a_out_norm, b_out_norm, w_out, lb_logits, ple_gate_norm, w_ple_gate,
           w_ple_proj, ple_post_norm, final_norm):
    depth = w_in.shape[0]
    assert x.shape[1] % TM_PROJ == 0 and x.shape[1] % T_MIX == 0 and x.shape[1] % SQ == 0 and T_MIX % CHUNK == 0
    h = x
    for i in range(depth):
        ya, bq, bkt, bv, bg = _mix_in(i, h, norm_mix[i], w_in, lb_logits, a_out_norm[i])
        yb = _stick(bq, bkt, bv, bg, b_norm_g=b_out_norm[i])
        h = _outproj(i, i == depth - 1, h, ya, yb, w_out, p, w_ple_proj, ple_post_norm[i],
                     ple_gate_norm[i], w_ple_gate, final_norm)
    return h
```

```python
import functools
import math

import jax
import jax.numpy as jnp
from jax import lax
from jax.experimental import pallas as pl
from jax.experimental.pallas import tpu as pltpu

D_MODEL = 1024
D_PLE = 256
A_HEADS = 4
A_DK = 128
A_WIDTH = A_HEADS * A_DK
B_HEADS = 8
B_DH = 64
B_WIDTH = B_HEADS * B_DH
GROUP = 512
N_GROUPS = 8
CHUNK = 64
SUB = 16
EPS = 1e-6
LOG2E = math.log2(math.e)

F32 = jnp.float32
BF16 = jnp.bfloat16

TM_PROJ = 512
ROWS_OUT = 256
T_MIX = 512
SQ = 512
KB = 128
KG = SQ // KB
VMEM_LIMIT = 56 * 1024 * 1024


def _sigmoid(x):
    return 1.0 / (1.0 + jnp.exp(-x))


def _silu(x):
    return x * _sigmoid(x)


def _rms(x, g):
    return x * lax.rsqrt(jnp.mean(x * x, axis=-1, keepdims=True) + EPS) * g


def _mix_in_kernel(layer, x_ref, g_ref, w_ref, lbl_ref, ga_ref, ya_ref, bq_ref, bkt_ref, bv_ref, bg_ref, st_ref):
    @pl.when(pl.program_id(1) == 0)
    def _():
        st_ref[...] = jnp.zeros_like(st_ref)

    u = _rms(x_ref[0], g_ref[...]).astype(BF16)

    def proj(g):
        return jnp.dot(u, w_ref[0, :, g * GROUP:(g + 1) * GROUP].astype(BF16), preferred_element_type=F32)

    aq = _silu(proj(0))
    af = proj(1)
    ai = proj(2).astype(BF16)
    ag = _silu(proj(3))

    logits = lbl_ref[...]
    e = jnp.exp(logits - jnp.max(logits, axis=0, keepdims=True))
    sm = e / jnp.sum(e, axis=0, keepdims=True)
    lb = jnp.sum(sm[:layer + 1], axis=0, keepdims=True) - sm[0:1]
    one_minus_lb = 1.0 - lb
    gain = ga_ref[...]

    r_i = lax.broadcasted_iota(jnp.int32, (CHUNK, CHUNK), 0)
    c_i = lax.broadcasted_iota(jnp.int32, (CHUNK, CHUNK), 1)
    tril = jnp.where(c_i <= r_i, 1.0, 0.0).astype(BF16)
    sub_masks = []
    for i in range(CHUNK // SUB):
        n = SUB * (i + 1)
        t_idx = SUB * i + lax.broadcasted_iota(jnp.int32, (SUB, n), 0)
        s_idx = lax.broadcasted_iota(jnp.int32, (SUB, n), 1)
        sub_masks.append(s_idx <= t_idx)

    items = [(h, c) for h in range(A_HEADS) for c in range(T_MIX // CHUNK)]
    n_sub = CHUNK // SUB
    nt = (((1,), (1,)), ((), ()))

    def rows_of(c):
        return slice(c * CHUNK, (c + 1) * CHUNK)

    def cols_of(h):
        return slice(h * A_DK, (h + 1) * A_DK)

    q, v, ka, hi_lo = {}, {}, {}, {}
    for h, c in items:
        q[h, c] = aq[rows_of(c), cols_of(h)]
        v[h, c] = ai[rows_of(c), cols_of(h)]
        ka[h, c] = one_minus_lb[:, cols_of(h)] / (1.0 + jnp.exp(af[rows_of(c), cols_of(h)]))
        logf = jnp.log(1.0 - ka[h, c])
        hi = logf.astype(BF16)
        lo = (logf - hi.astype(F32)).astype(BF16)
        hi_lo[h, c] = jnp.concatenate([hi, lo], axis=0)

    bq_ref[0] = (proj(4) * (B_DH ** -0.5 * LOG2E)).astype(BF16)

    tril2 = jnp.concatenate([tril, tril], axis=1)
    b = {k: jnp.dot(tril2, hi_lo[k], preferred_element_type=F32) for k in items}

    bkt_ref[0] = proj(5).T.astype(BF16)

    qe, kd, decay, qs, ks = {}, {}, {}, {}, {}
    for k in items:
        b_last = b[k][CHUNK - 1:CHUNK, :]
        qe[k] = (q[k] * jnp.exp(b[k])).astype(BF16)
        kd[k] = (ka[k] * jnp.exp(b_last - b[k])).astype(BF16)
        decay[k] = jnp.exp(b_last)
        for i in range(n_sub):
            n = SUB * (i + 1)
            ref_row = b[k][SUB * i + SUB // 2 - 1:SUB * i + SUB // 2, :]
            qs[k, i] = (q[k][SUB * i:n] * jnp.exp(b[k][SUB * i:n] - ref_row)).astype(BF16)
            ks[k, i] = (ka[k][0:n] * jnp.exp(ref_row - b[k][0:n])).astype(BF16)
    a = {}
    for k in items:
        for i in range(n_sub):
            s = lax.dot_general(qs[k, i], ks[k, i], nt, preferred_element_type=F32)
            a[k, i] = jnp.where(sub_masks[i], s, 0.0).astype(BF16)

    bv_ref[0] = proj(6).astype(BF16)

    o_intra, upd = {}, {}
    for k in items:
        o_intra[k] = jnp.concatenate(
            [jnp.dot(a[k, i], v[k][0:SUB * (i + 1)], preferred_element_type=F32) for i in range(n_sub)], axis=0)
        vt = v[k].astype(F32).T.astype(BF16)
        upd[k] = jnp.dot(vt, kd[k], preferred_element_type=F32)

    bg_ref[0] = _silu(proj(7)).astype(BF16)

    st = [st_ref[h] for h in range(A_HEADS)]
    for c in range(T_MIX // CHUNK):
        for h in range(A_HEADS):
            o = o_intra[h, c] + lax.dot_general(qe[h, c], st[h].astype(BF16), nt, preferred_element_type=F32)
            st[h] = st[h] * decay[h, c] + upd[h, c]
            o = o * lax.rsqrt(jnp.mean(o * o, axis=-1, keepdims=True) + EPS) * gain[:, cols_of(h)]
            ya_ref[0, rows_of(c), cols_of(h)] = (o * ag[rows_of(c), cols_of(h)]).astype(BF16)
    for h in range(A_HEADS):
        st_ref[h] = st[h]


def _mix_in(layer, h, norm_g, w_in, lb_logits, a_norm_g):
    B, S, D = h.shape
    depth = lb_logits.shape[0]
    tok = jax.ShapeDtypeStruct((B, S, GROUP), BF16)
    row_spec = pl.BlockSpec((1, T_MIX, GROUP), lambda b, i: (b, i, 0))
    full = lambda shape: pl.BlockSpec(shape, lambda b, i: (0, 0))
    return pl.pallas_call(
        functools.partial(_mix_in_kernel, layer),
        grid=(B, S // T_MIX),
        in_specs=[pl.BlockSpec((1, T_MIX, D), lambda b, i: (b, i, 0)),
                  full((1, D)),
                  pl.BlockSpec((1, D, N_GROUPS * GROUP), lambda b, i: (layer, 0, 0), pipeline_mode=pl.Buffered(1)),
                  full((depth, A_WIDTH)), full((1, A_WIDTH))],
        out_specs=[row_spec, row_spec,
                   pl.BlockSpec((1, GROUP, T_MIX), lambda b, i: (b, 0, i)),
                   row_spec, row_spec],
        out_shape=[tok, tok, jax.ShapeDtypeStruct((B, GROUP, S), BF16), tok, tok],
        scratch_shapes=[pltpu.VMEM((A_HEADS, A_DK, A_DK), F32)],
        compiler_params=pltpu.CompilerParams(
            dimension_semantics=("parallel", "arbitrary"), vmem_limit_bytes=VMEM_LIMIT),
        name="mix_in",
    )(h, norm_g.reshape(1, D), w_in, lb_logits, a_norm_g.reshape(1, A_WIDTH))


def _stick_kernel(q_ref, kt_ref, v_ref, gate_ref, g_ref, y_ref, run_ref, acc_ref):
    qi = pl.program_id(2)
    heads = range(2)
    pairs = tuple(reversed(range(KG // 2)))
    lane = lax.broadcasted_iota(jnp.int32, (SQ, 2 * B_DH), 1)
    first = lane < B_DH
    q2 = q_ref[0]
    zero = jnp.zeros_like(q2)
    q_heads = (jnp.where(first, q2, zero), jnp.where(first, zero, q2))

    src_key = lax.broadcasted_iota(jnp.int32, (2 * KB, 2 * KB), 0)
    dst_key = lax.broadcasted_iota(jnp.int32, (2 * KB, 2 * KB), 1)
    w_cum = jnp.where(src_key >= dst_key, -1.0, 0.0).astype(BF16)

    row = lax.broadcasted_iota(jnp.int32, (SQ, 2 * KB), 0)
    col = lax.broadcasted_iota(jnp.int32, (SQ, 2 * KB), 1)

    def visit(groups, first_is_diagonal):
        z = {}

        def first_row(g, p):
            return 2 * p * KB if first_is_diagonal and g == 0 else 0

        def logits(g, p, h):
            k0 = pl.multiple_of((groups[g] + 2 * p) * KB, 2 * KB)
            kt = kt_ref[0, :, pl.ds(k0, 2 * KB)]
            z[g, p, h] = jnp.dot(q_heads[h][first_row(g, p):], kt, preferred_element_type=F32)

        for p in pairs:
            for h in heads:
                logits(0, p, h)
        if first_is_diagonal:
            run = [jnp.zeros((SQ, KB), F32) for h in heads]
            acc = [jnp.zeros((SQ, 2 * B_DH), F32) for h in heads]
        else:
            run = [run_ref[h] for h in heads]
            acc = [acc_ref[h] for h in heads]
        for g, j0 in enumerate(groups):
            diagonal = first_is_diagonal and g == 0
            w = {}
            for p in pairs:
                r0 = first_row(g, p)
                mask = (col + (2 * p * KB) < row)[r0:] if diagonal else None
                cum, tot = [None, None], [None, None]
                for h in heads:
                    zz = z[g, p, h]
                    sp = jnp.maximum(zz, 0.0) + jnp.log(1.0 + jnp.exp2(-jnp.abs(zz))) * LOG2E
                    if diagonal:
                        sp = jnp.where(mask, sp, 0.0)
                    tot[h] = jnp.sum(sp, axis=-1, keepdims=True)
                    cum[h] = jnp.dot(sp.astype(BF16), w_cum, preferred_element_type=F32)
                    if g + 1 < len(groups):
                        logits(g + 1, p, h)
                for h in heads:
                    live = run[h][r0:]
                    wt = jnp.exp2(z[g, p, h] + cum[h] + jnp.concatenate([live, live], axis=1))
                    if diagonal:
                        wt = jnp.where(mask, wt, 0.0)
                    wt = wt.astype(BF16)
                    live = live - tot[h]
                    if r0:
                        wt = jnp.concatenate([jnp.zeros((r0, 2 * KB), BF16), wt], axis=0)
                        live = jnp.concatenate([run[h][:r0], live], axis=0)
                    w[p, h] = wt
                    run[h] = live
            k0 = pl.multiple_of(j0 * KB, KG * KB)
            v = v_ref[0, pl.ds(k0, KG * KB), :]
            for h in heads:
                w_all = jnp.concatenate([w[p, h] for p in sorted(pairs)], axis=1)
                acc[h] = acc[h] + jnp.dot(w_all, v, preferred_element_type=F32)
        for h in heads:
            run_ref[h] = run[h]
            acc_ref[h] = acc[h]

    odd = qi % 2
    merged = jnp.where(qi == 0, 0, 2 - odd)

    @pl.when(qi == 0)
    def _():
        visit([qi * KG], True)

    @pl.when(odd == 1)
    def _():
        visit([qi * KG, (qi - 1) * KG], True)

    @pl.when((odd == 0) & (qi > 0))
    def _():
        visit([qi * KG, (qi - 1) * KG, (qi - 2) * KG], True)

    top = qi - 1 - merged

    def body(i, carry):
        visit([(top - 2 * i) * KG, (top - 2 * i - 1) * KG], False)
        return carry

    lax.fori_loop(0, (qi - merged) // 2, body, 0)

    o = jnp.where(first, acc_ref[0], acc_ref[1])
    o2 = o * o
    ms0 = jnp.sum(jnp.where(first, o2, 0.0), axis=-1, keepdims=True)
    ms1 = jnp.sum(jnp.where(first, 0.0, o2), axis=-1, keepdims=True)
    ms = jnp.where(first, ms0, ms1) * (1.0 / B_DH)
    y = o * lax.rsqrt(ms + EPS) * g_ref[...] * gate_ref[0].astype(F32)
    y_ref[0] = y.astype(BF16)


def _stick(bq, bkt, bv, bg, b_norm_g):
    B, S, W = bq.shape
    pairs = W // (2 * B_DH)
    blk = pl.BlockSpec((1, SQ, 2 * B_DH), lambda b, p, i: (b, i, p))
    return pl.pallas_call(
        _stick_kernel,
        grid=(B, pairs, S // SQ),
        in_specs=[blk,
                  pl.BlockSpec((1, 2 * B_DH, S), lambda b, p, i: (b, p, 0)),
                  pl.BlockSpec((1, S, 2 * B_DH), lambda b, p, i: (b, 0, p)),
                  blk,
                  pl.BlockSpec((1, 2 * B_DH), lambda b, p, i: (0, p))],
        out_specs=blk,
        out_shape=jax.ShapeDtypeStruct((B, S, W), BF16),
        scratch_shapes=[pltpu.VMEM((2, SQ, KB), F32), pltpu.VMEM((2, SQ, 2 * B_DH), F32)],
        compiler_params=pltpu.CompilerParams(
            dimension_semantics=("parallel", "parallel", "arbitrary"), vmem_limit_bytes=VMEM_LIMIT),
        name="stick",
    )(bq, bkt, bv, bg, b_norm_g.reshape(1, W))


def _outproj_kernel(last, h_ref, ya_ref, yb_ref, wo_ref, p_ref, wp_ref, post_ref, gn_ref, wg_ref, fin_ref, o_ref):
    chunks = [slice(r * ROWS_OUT, (r + 1) * ROWS_OUT) for r in range(TM_PROJ // ROWS_OUT)]
    wo_a = wo_ref[0, :A_WIDTH, :].astype(BF16)
    wo_b = wo_ref[0, A_WIDTH:, :].astype(BF16)
    wp = wp_ref[0].astype(BF16)
    wg = wg_ref[0].astype(BF16)
    h1, pe, gate = [], [], []
    for rows in chunks:
        h1.append(h_ref[0, rows, :]
                  + jnp.dot(ya_ref[0, rows, :], wo_a, preferred_element_type=F32)
                  + jnp.dot(yb_ref[0, rows, :], wo_b, preferred_element_type=F32))
        pe.append(jnp.dot(p_ref[0, 0, rows, :].astype(BF16), wp, preferred_element_type=F32))
    for r in range(len(chunks)):
        normed = _rms(h1[r], gn_ref[...]).astype(BF16)
        gate.append(jnp.dot(normed, wg, preferred_element_type=F32))
    for r, rows in enumerate(chunks):
        h2 = h1[r] + _sigmoid(gate[r]) * _rms(pe[r], post_ref[...])
        if last:
            h2 = _rms(h2, fin_ref[...])
        o_ref[0, rows, :] = h2


def _outproj(layer, last, h, ya, yb, w_out, p, w_proj, post_g, gate_g, w_gate, final_g):
    B, S, D = h.shape
    row = lambda w: pl.BlockSpec((1, TM_PROJ, w), lambda b, i: (b, i, 0))
    full = lambda a: pl.BlockSpec(a.shape, lambda b, i: (0, 0))
    slab = lambda a: pl.BlockSpec((1,) + a.shape[1:], lambda b, i: (layer, 0, 0), pipeline_mode=pl.Buffered(1))
    vec = lambda a: a.reshape(1, D)
    args = (h, ya, yb, w_out, p, w_proj, vec(post_g), vec(gate_g), w_gate, vec(final_g))
    p_spec = pl.BlockSpec((1, 1, TM_PROJ, D_PLE), lambda b, i: (layer, b, i, 0))
    in_specs = [row(D), row(A_WIDTH), row(B_WIDTH), slab(w_out), p_spec, slab(w_proj),
                full(args[6]), full(args[7]), slab(w_gate), full(args[9])]
    return pl.pallas_call(
        functools.partial(_outproj_kernel, last),
        grid=(B, S // TM_PROJ),
        in_specs=in_specs,
        out_specs=row(D),
        out_shape=jax.ShapeDtypeStruct((B, S, D), F32),
        compiler_params=pltpu.CompilerParams(
            dimension_semantics=("parallel", "parallel"), vmem_limit_bytes=VMEM_LIMIT),
        name="outproj",
    )(*args)


def kernel(x, p, norm_mix, w_in, a_out_norm, b_out_norm, w_out, lb_logits, ple_gate_norm, w_ple_gate,
           w_ple_proj, ple_post_norm, final_norm):
    depth = w_in.shape[0]
    assert x.shape[1] % TM_PROJ == 0 and x.shape[1] % T_MIX == 0 and x.shape[1] % SQ == 0 and T_MIX % CHUNK == 0
    h = x
    for i in range(depth):
        ya, bq, bkt, bv, bg = _mix_in(i, h, norm_mix[i], w_in, lb_logits, a_out_norm[i])
        yb = _stick(bq, bkt, bv, bg, b_norm_g=b_out_norm[i])
        h = _outproj(i, i == depth - 1, h, ya, yb, w_out, p, w_ple_proj, ple_post_norm[i],
                     ple_gate_norm[i], w_ple_gate, final_norm)
    return h
```

```python
import functools
import math

import jax
import jax.numpy as jnp
from jax import lax
from jax.experimental import pallas as pl
from jax.experimental.pallas import tpu as pltpu

D_MODEL = 1024
D_PLE = 256
A_HEADS = 4
A_DK = 128
A_WIDTH = A_HEADS * A_DK
B_HEADS = 8
B_DH = 64
B_WIDTH = B_HEADS * B_DH
GROUP = 512
N_GROUPS = 8
CHUNK = 64
SUB = 16
EPS = 1e-6
LOG2E = math.log2(math.e)

F32 = jnp.float32
BF16 = jnp.bfloat16

TM_PROJ = 512
ROWS_OUT = 256
T_MIX = 512
SQ = 512
KB = 128
KG = SQ // KB
VMEM_LIMIT = 56 * 1024 * 1024


def _sigmoid(x):
    return 1.0 / (1.0 + jnp.exp(-x))


def _silu(x):
    return x * _sigmoid(x)


def _rms(x, g):
    return x * lax.rsqrt(jnp.mean(x * x, axis=-1, keepdims=True) + EPS) * g


def _mix_in_kernel(layer, x_ref, g_ref, w_ref, lbl_ref, ga_ref, ya_ref, bq_ref, bkt_ref, bv_ref, bg_ref, st_ref):
    @pl.when(pl.program_id(1) == 0)
    def _():
        st_ref[...] = jnp.zeros_like(st_ref)

    u = _rms(x_ref[0], g_ref[...]).astype(BF16)

    def proj(g):
        return jnp.dot(u, w_ref[0, :, g * GROUP:(g + 1) * GROUP].astype(BF16), preferred_element_type=F32)

    aq = _silu(proj(0))
    af = proj(1)
    ai = proj(2).astype(BF16)
    ag = _silu(proj(3))

    logits = lbl_ref[...]
    e = jnp.exp(logits - jnp.max(logits, axis=0, keepdims=True))
    sm = e / jnp.sum(e, axis=0, keepdims=True)
    lb = jnp.sum(sm[:layer + 1], axis=0, keepdims=True) - sm[0:1]
    one_minus_lb = 1.0 - lb
    gain = ga_ref[...]

    r_i = lax.broadcasted_iota(jnp.int32, (CHUNK, CHUNK), 0)
    c_i = lax.broadcasted_iota(jnp.int32, (CHUNK, CHUNK), 1)
    tril = jnp.where(c_i <= r_i, 1.0, 0.0).astype(BF16)
    sub_masks = []
    for i in range(CHUNK // SUB):
        n = SUB * (i + 1)
        t_idx = SUB * i + lax.broadcasted_iota(jnp.int32, (SUB, n), 0)
        s_idx = lax.broadcasted_iota(jnp.int32, (SUB, n), 1)
        sub_masks.append(s_idx <= t_idx)

    items = [(h, c) for h in range(A_HEADS) for c in range(T_MIX // CHUNK)]
    n_sub = CHUNK // SUB
    nt = (((1,), (1,)), ((), ()))

    def rows_of(c):
        return slice(c * CHUNK, (c + 1) * CHUNK)

    def cols_of(h):
        return slice(h * A_DK, (h + 1) * A_DK)

    q, v, ka, hi_lo = {}, {}, {}, {}
    for h, c in items:
        q[h, c] = aq[rows_of(c), cols_of(h)]
        v[h, c] = ai[rows_of(c), cols_of(h)]
        ka[h, c] = one_minus_lb[:, cols_of(h)] / (1.0 + jnp.exp(af[rows_of(c), cols_of(h)]))
        logf = jnp.log(1.0 - ka[h, c])
        hi = logf.astype(BF16)
        lo = (logf - hi.astype(F32)).astype(BF16)
        hi_lo[h, c] = jnp.concatenate([hi, lo], axis=0)

    bq_ref[0] = (proj(4) * (B_DH ** -0.5 * LOG2E)).astype(BF16)

    tril2 = jnp.concatenate([tril, tril], axis=1)
    b = {k: jnp.dot(tril2, hi_lo[k], preferred_element_type=F32) for k in items}

    bkt_ref[0] = proj(5).T.astype(BF16)

    qe, kd, decay, qs, ks = {}, {}, {}, {}, {}
    for k in items:
        b_last = b[k][CHUNK - 1:CHUNK, :]
        qe[k] = (q[k] * jnp.exp(b[k])).astype(BF16)
        kd[k] = (ka[k] * jnp.exp(b_last - b[k])).astype(BF16)
        decay[k] = jnp.exp(b_last)
        for i in range(n_sub):
            n = SUB * (i + 1)
            ref_row = b[k][SUB * i + SUB // 2 - 1:SUB * i + SUB // 2, :]
            qs[k, i] = (q[k][SUB * i:n] * jnp.exp(b[k][SUB * i:n] - ref_row)).astype(BF16)
            ks[k, i] = (ka[k][0:n] * jnp.exp(ref_row - b[k][0:n])).astype(BF16)
    a = {}
    for k in items:
        for i in range(n_sub):
            s = lax.dot_general(qs[k, i], ks[k, i], nt, preferred_element_type=F32)
            a[k, i] = jnp.where(sub_masks[i], s, 0.0).astype(BF16)

    bv_ref[0] = proj(6).astype(BF16)

    o_intra, upd = {}, {}
    for k in items:
        o_intra[k] = jnp.concatenate(
            [jnp.dot(a[k, i], v[k][0:SUB * (i + 1)], preferred_element_type=F32) for i in range(n_sub)], axis=0)
        vt = v[k].astype(F32).T.astype(BF16)
        upd[k] = jnp.dot(vt, kd[k], preferred_element_type=F32)

    bg_ref[0] = _silu(proj(7)).astype(BF16)

    st = [st_ref[h] for h in range(A_HEADS)]
    for c in range(T_MIX // CHUNK):
        for h in range(A_HEADS):
            o = o_intra[h, c] + lax.dot_general(qe[h, c], st[h].astype(BF16), nt, preferred_element_type=F32)
            st[h] = st[h] * decay[h, c] + upd[h, c]
            o = o * lax.rsqrt(jnp.mean(o * o, axis=-1, keepdims=True) + EPS) * gain[:, cols_of(h)]
            ya_ref[0, rows_of(c), cols_of(h)] = (o * ag[rows_of(c), cols_of(h)]).astype(BF16)
    for h in range(A_HEADS):
        st_ref[h] = st[h]


def _mix_in(layer, h, norm_g, w_in, lb_logits, a_norm_g):
    B, S, D = h.shape
    depth = lb_logits.shape[0]
    tok = jax.ShapeDtypeStruct((B, S, GROUP), BF16)
    row_spec = pl.BlockSpec((1, T_MIX, GROUP), lambda b, i: (b, i, 0))
    full = lambda shape: pl.BlockSpec(shape, lambda b, i: (0, 0))
    return pl.pallas_call(
        functools.partial(_mix_in_kernel, layer),
        grid=(B, S // T_MIX),
        in_specs=[pl.BlockSpec((1, T_MIX, D), lambda b, i: (b, i, 0)),
                  full((1, D)),
                  pl.BlockSpec((1, D, N_GROUPS * GROUP), lambda b, i: (layer, 0, 0), pipeline_mode=pl.Buffered(1)),
                  full((depth, A_WIDTH)), full((1, A_WIDTH))],
        out_specs=[row_spec, row_spec,
                   pl.BlockSpec((1, GROUP, T_MIX), lambda b, i: (b, 0, i)),
                   row_spec, row_spec],
        out_shape=[tok, tok, jax.ShapeDtypeStruct((B, GROUP, S), BF16), tok, tok],
        scratch_shapes=[pltpu.VMEM((A_HEADS, A_DK, A_DK), F32)],
        compiler_params=pltpu.CompilerParams(
            dimension_semantics=("parallel", "arbitrary"), vmem_limit_bytes=VMEM_LIMIT),
        name="mix_in",
    )(h, norm_g.reshape(1, D), w_in, lb_logits, a_norm_g.reshape(1, A_WIDTH))


def _stick_kernel(q_ref, kt_ref, v_ref, gate_ref, g_ref, y_ref, run_ref, acc_ref):
    qi = pl.program_id(2)
    heads = range(2)
    pairs = tuple(reversed(range(KG // 2)))
    lane = lax.broadcasted_iota(jnp.int32, (SQ, 2 * B_DH), 1)
    first = lane < B_DH
    q2 = q_ref[0]
    zero = jnp.zeros_like(q2)
    q_heads = (jnp.where(first, q2, zero), jnp.where(first, zero, q2))

    src_key = lax.broadcasted_iota(jnp.int32, (2 * KB, 2 * KB), 0)
    dst_key = lax.broadcasted_iota(jnp.int32, (2 * KB, 2 * KB), 1)
    w_cum = jnp.where(src_key >= dst_key, -1.0, 0.0).astype(BF16)

    row = lax.broadcasted_iota(jnp.int32, (SQ, 2 * KB), 0)
    col = lax.broadcasted_iota(jnp.int32, (SQ, 2 * KB), 1)

    def visit(groups, first_is_diagonal):
        z = {}

        def first_row(g, p):
            return 2 * p * KB if first_is_diagonal and g == 0 else 0

        def logits(g, p, h):
            k0 = pl.multiple_of((groups[g] + 2 * p) * KB, 2 * KB)
            kt = kt_ref[0, :, pl.ds(k0, 2 * KB)]
            z[g, p, h] = jnp.dot(q_heads[h][first_row(g, p):], kt, preferred_element_type=F32)

        for p in pairs:
            for h in heads:
                logits(0, p, h)
        if first_is_diagonal:
            run = [jnp.zeros((SQ, KB), F32) for h in heads]
            acc = [jnp.zeros((SQ, 2 * B_DH), F32) for h in heads]
        else:
            run = [run_ref[h] for h in heads]
            acc = [acc_ref[h] for h in heads]
        for g, j0 in enumerate(groups):
            diagonal = first_is_diagonal and g == 0
            w = {}
            for p in pairs:
                r0 = first_row(g, p)
                mask = (col + (2 * p * KB) < row)[r0:] if diagonal else None
                cum, tot = [None, None], [None, None]
                for h in heads:
                    zz = z[g, p, h]
                    sp = jnp.maximum(zz, 0.0) + jnp.log(1.0 + jnp.exp2(-jnp.abs(zz))) * LOG2E
                    if diagonal:
                        sp = jnp.where(mask, sp, 0.0)
                    tot[h] = jnp.sum(sp, axis=-1, keepdims=True)
                    cum[h] = jnp.dot(sp.astype(BF16), w_cum, preferred_element_type=F32)
                    if g + 1 < len(groups):
                        logits(g + 1, p, h)
                for h in heads:
                    live = run[h][r0:]
                    wt = jnp.exp2(z[g, p, h] + cum[h] + jnp.concatenate([live, live], axis=1))
                    if diagonal:
                        wt = jnp.where(mask, wt, 0.0)
                    wt = wt.astype(BF16)
                    live = live - tot[h]
                    if r0:
                        wt = jnp.concatenate([jnp.zeros((r0, 2 * KB), BF16), wt], axis=0)
                        live = jnp.concatenate([run[h][:r0], live], axis=0)
                    w[p, h] = wt
                    run[h] = live
            k0 = pl.multiple_of(j0 * KB, KG * KB)
            v = v_ref[0, pl.ds(k0, KG * KB), :]
            for h in heads:
                w_all = jnp.concatenate([w[p, h] for p in sorted(pairs)], axis=1)
                acc[h] = acc[h] + jnp.dot(w_all, v, preferred_element_type=F32)
        for h in heads:
            run_ref[h] = run[h]
            acc_ref[h] = acc[h]

    odd = qi % 2
    merged = jnp.where(qi <= 1, qi, 2 + odd)

    @pl.when(qi == 0)
    def _():
        visit([qi * KG], True)

    @pl.when(qi == 1)
    def _():
        visit([qi * KG, (qi - 1) * KG], True)

    @pl.when((odd == 0) & (qi > 1))
    def _():
        visit([qi * KG, (qi - 1) * KG, (qi - 2) * KG], True)

    @pl.when((odd == 1) & (qi > 1))
    def _():
        visit([qi * KG, (qi - 1) * KG, (qi - 2) * KG, (qi - 3) * KG], True)

    top = qi - 1 - merged

    def body(i, carry):
        visit([(top - 2 * i) * KG, (top - 2 * i - 1) * KG], False)
        return carry

    lax.fori_loop(0, (qi - merged) // 2, body, 0)

    o = jnp.where(first, acc_ref[0], acc_ref[1])
    o2 = o * o
    ms0 = jnp.sum(jnp.where(first, o2, 0.0), axis=-1, keepdims=True)
    ms1 = jnp.sum(jnp.where(first, 0.0, o2), axis=-1, keepdims=True)
    ms = jnp.where(first, ms0, ms1) * (1.0 / B_DH)
    y = o * lax.rsqrt(ms + EPS) * g_ref[...] * gate_ref[0].astype(F32)
    y_ref[0] = y.astype(BF16)


def _stick(bq, bkt, bv, bg, b_norm_g):
    B, S, W = bq.shape
    pairs = W // (2 * B_DH)
    blk = pl.BlockSpec((1, SQ, 2 * B_DH), lambda b, p, i: (b, i, p))
    return pl.pallas_call(
        _stick_kernel,
        grid=(B, pairs, S // SQ),
        in_specs=[blk,
                  pl.BlockSpec((1, 2 * B_DH, S), lambda b, p, i: (b, p, 0)),
                  pl.BlockSpec((1, S, 2 * B_DH), lambda b, p, i: (b, 0, p)),
                  blk,
                  pl.BlockSpec((1, 2 * B_DH), lambda b, p, i: (0, p))],
        out_specs=blk,
        out_shape=jax.ShapeDtypeStruct((B, S, W), BF16),
        scratch_shapes=[pltpu.VMEM((2, SQ, KB), F32), pltpu.VMEM((2, SQ, 2 * B_DH), F32)],
        compiler_params=pltpu.CompilerParams(
            dimension_semantics=("parallel", "parallel", "arbitrary"), vmem_limit_bytes=VMEM_LIMIT),
        name="stick",
    )(bq, bkt, bv, bg, b_norm_g.reshape(1, W))


def _outproj_kernel(last, h_ref, ya_ref, yb_ref, wo_ref, p_ref, wp_ref, post_ref, gn_ref, wg_ref, fin_ref, o_ref):
    chunks = [slice(r * ROWS_OUT, (r + 1) * ROWS_OUT) for r in range(TM_PROJ // ROWS_OUT)]
    wo_a = wo_ref[0, :A_WIDTH, :].astype(BF16)
    wo_b = wo_ref[0, A_WIDTH:, :].astype(BF16)
    wp = wp_ref[0].astype(BF16)
    wg = wg_ref[0].astype(BF16)
    h1, pe, gate = [], [], []
    for rows in chunks:
        h1.append(h_ref[0, rows, :]
                  + jnp.dot(ya_ref[0, rows, :], wo_a, preferred_element_type=F32)
                  + jnp.dot(yb_ref[0, rows, :], wo_b, preferred_element_type=F32))
        pe.append(jnp.dot(p_ref[0, 0, rows, :].astype(BF16), wp, preferred_element_type=F32))
    for r in range(len(chunks)):
        normed = _rms(h1[r], gn_ref[...]).astype(BF16)
        gate.append(jnp.dot(normed, wg, preferred_element_type=F32))
    for r, rows in enumerate(chunks):
        h2 = h1[r] + _sigmoid(gate[r]) * _rms(pe[r], post_ref[...])
        if last:
            h2 = _rms(h2, fin_ref[...])
        o_ref[0, rows, :] = h2


def _outproj(layer, last, h, ya, yb, w_out, p, w_proj, post_g, gate_g, w_gate, final_g):
    B, S, D = h.shape
    row = lambda w: pl.BlockSpec((1, TM_PROJ, w), lambda b, i: (b, i, 0))
    full = lambda a: pl.BlockSpec(a.shape, lambda b, i: (0, 0))
    slab = lambda a: pl.BlockSpec((1,) + a.shape[1:], lambda b, i: (layer, 0, 0), pipeline_mode=pl.Buffered(1))
    vec = lambda a: a.reshape(1, D)
    args = (h, ya, yb, w_out, p, w_proj, vec(post_g), vec(gate_g), w_gate, vec(final_g))
    p_spec = pl.BlockSpec((1, 1, TM_PROJ, D_PLE), lambda b, i: (layer, b, i, 0))
    in_specs = [row(D), row(A_WIDTH), row(B_WIDTH), slab(w_out), p_spec, slab(w_proj),
                full(args[6]), full(args[7]), slab(w_gate), full(args[9])]
    return pl.pallas_call(
        functools.partial(_outproj_kernel, last),
        grid=(B, S // TM_PROJ),
        in_specs=in_specs,
        out_specs=row(D),
        out_shape=jax.ShapeDtypeStruct((B, S, D), F32),
        compiler_params=pltpu.CompilerParams(
            dimension_semantics=("parallel", "parallel"), vmem_limit_bytes=VMEM_LIMIT),
        name="outproj",
    )(*args)


def kernel(x, p, norm_mix, w_in, a_out_norm, b_out_norm, w_out, lb_logits, ple_gate_norm, w_ple_gate,
           w_ple_proj, ple_post_norm, final_norm):
    depth = w_in.shape[0]
    assert x.shape[1] % TM_PROJ == 0 and x.shape[1] % T_MIX == 0 and x.shape[1] % SQ == 0 and T_MIX % CHUNK == 0
    h = x
    for i in range(depth):
        ya, bq, bkt, bv, bg = _mix_in(i, h, norm_mix[i], w_in, lb_logits, a_out_norm[i])
        yb = _stick(bq, bkt, bv, bg, b_norm_g=b_out_norm[i])
        h = _outproj(i, i == depth - 1, h, ya, yb, w_out, p, w_ple_proj, ple_post_norm[i],
                     ple_gate_norm[i], w_ple_gate, final_norm)
    return h
```
